```python
import math
import jax, jax.numpy as jnp
from jax import lax
import numpy as np

D_MODEL = 1024
BATCH = 8
SEQ = 2048
DEPTH = 2
DEC_BATCH = 128
DEC_SEQ = 8
PAST_LEN = 16384
PAGE_SIZE = 128

D_RNN = D_MODEL
N_RNN_HEADS = 16
RNN_BLOCK = D_RNN // N_RNN_HEADS
RG_CONV_W = 4
RG_C = 8.0
D_CONV = D_MODEL
CC_CONV_W = 31
D_FF = 2816
N_EXPERTS = 8
TOP_K = 2
D_FF_EXPERT = 1408
N_DENSE = (DEPTH + 1) // 2
N_MOE = DEPTH // 2
D_IN = 2 * D_RNN + 2 * D_CONV + 2 * D_MODEL
EPS = 1e-6

kernel_name = "hybrid_rglru_conformer_adaln_moe_step"


def rmsnorm(x, g):
    xf = x.astype(jnp.float32)
    y = xf * lax.rsqrt(jnp.mean(xf * xf, axis=-1, keepdims=True) + EPS)
    return (y * g.astype(jnp.float32)).astype(x.dtype)


def layernorm(x, g, b):
    xf = x.astype(jnp.float32)
    mu = jnp.mean(xf, axis=-1, keepdims=True)
    xc = xf - mu
    y = xc * lax.rsqrt(jnp.mean(xc * xc, axis=-1, keepdims=True) + EPS)
    return (y * g.astype(jnp.float32) + b.astype(jnp.float32)).astype(x.dtype)


def causal_depthwise_conv(u, buf, w, b):
    K, C = w.shape
    full = jnp.concatenate([buf.astype(u.dtype), u], axis=1)
    y = lax.conv_general_dilated(full, w.astype(u.dtype)[:, None, :], window_strides=(1,),
                                 padding="VALID", dimension_numbers=("NWC", "WIO", "NWC"),
                                 feature_group_count=C)
    return y + b.astype(u.dtype), full[:, full.shape[1] - (K - 1):]


def rg_lru(x, h0, wa, ba, wx, bx, lam):
    B, T, Dr = x.shape
    xf = x.astype(jnp.float32)
    xh = xf.reshape(B, T, N_RNN_HEADS, RNN_BLOCK)
    r = jax.nn.sigmoid(jnp.einsum("bthi,hij->bthj", xh, wa.astype(jnp.float32)).reshape(B, T, Dr) + ba.astype(jnp.float32))
    i = jax.nn.sigmoid(jnp.einsum("bthi,hij->bthj", xh, wx.astype(jnp.float32)).reshape(B, T, Dr) + bx.astype(jnp.float32))
    log_a = -RG_C * r * jax.nn.softplus(-lam.astype(jnp.float32))
    a = jnp.exp(log_a)
    bterm = jnp.sqrt(-jnp.expm1(2.0 * log_a)) * (i * xf)
    bterm = bterm.at[:, 0].add(a[:, 0] * h0.astype(jnp.float32))

    def combine(left, right):
        a1, b1 = left
        a2, b2 = right
        return a1 * a2, a2 * b1 + b2

    _, h = lax.associative_scan(combine, (a, bterm), axis=1)
    return h.astype(x.dtype), h[:, -1].astype(x.dtype)


def swiglu(x, wg, wu, wd):
    return (jax.nn.silu(x @ wg) * (x @ wu)) @ wd


def moe_swiglu(x, wr, br, wg, wu, wd):
    B, T, D = x.shape
    xt = x.reshape(B * T, D)
    logits = xt.astype(jnp.float32) @ wr.astype(jnp.float32) + br.astype(jnp.float32)
    probs = jax.nn.softmax(logits, axis=-1)
    topv, topi = lax.top_k(probs, TOP_K)
    topv = topv / jnp.sum(topv, axis=-1, keepdims=True)
    comb = jnp.sum(jax.nn.one_hot(topi, N_EXPERTS, dtype=jnp.float32) * topv[..., None], axis=1)
    out = jnp.zeros((B * T, D), jnp.float32)
    for e in range(N_EXPERTS):
        out = out + comb[:, e:e + 1] * swiglu(xt, wg[e], wu[e], wd[e]).astype(jnp.float32)
    return out.astype(x.dtype).reshape(B, T, D)


def trunk(x, c, h_state, rc_state, cc_state, p):
    splits = [D_RNN, 2 * D_RNN, 2 * D_RNN + D_CONV, 2 * D_RNN + 2 * D_CONV, 2 * D_RNN + 2 * D_CONV + D_MODEL]
    new_h, new_rc, new_cc = [], [], []
    for l in range(DEPTH):
        mod = jax.nn.silu(c) @ p["ada_w"][l] + p["ada_b"][l]
        sh1, sc1, g1, sh2, sc2, g2 = jnp.split(mod[:, None, :], 6, axis=-1)
        hn = rmsnorm(x, p["norm1_g"][l]) * (1.0 + sc1) + sh1
        proj = hn @ p["w_in"][l]
        xr, yr, ga, gb, gate_a, gate_b = jnp.split(proj, splits, axis=-1)
        xc, rc_new = causal_depthwise_conv(xr, rc_state[l], p["rg_conv_w"][l], p["rg_conv_b"][l])
        hseq, h_last = rg_lru(xc, h_state[l], p["rg_wa"][l], p["rg_ba"][l], p["rg_wx"][l], p["rg_bx"][l], p["rg_lambda"][l])
        ya = (jax.nn.gelu(yr) * hseq) @ p["w_branch_a"][l]
        u = ga * jax.nn.sigmoid(gb)
        v, cc_new = causal_depthwise_conv(u, cc_state[l], p["cc_dw_w"][l], p["cc_dw_b"][l])
        v = jax.nn.silu(layernorm(v, p["cc_ln_g"][l], p["cc_ln_b"][l]))
        yb = v @ p["w_branch_b"][l] + p["b_branch_b"][l]
        m = jax.nn.sigmoid(gate_a) * ya + jax.nn.sigmoid(gate_b) * yb
        x = x + (1.0 + g1) * (m @ p["w_out"][l])
        hn2 = rmsnorm(x, p["norm2_g"][l]) * (1.0 + sc2) + sh2
        if l % 2 == 0:
            j = l // 2
            f = swiglu(hn2, p["ffn_w_gate"][j], p["ffn_w_up"][j], p["ffn_w_down"][j])
        else:
            j = l // 2
            f = moe_swiglu(hn2, p["moe_router_w"][j], p["moe_router_b"][j], p["moe_w_gate"][j],
                           p["moe_w_up"][j], p["moe_w_down"][j])
        x = x + (1.0 + g2) * f
        new_h.append(h_last)
        new_rc.append(rc_new)
        new_cc.append(cc_new)
    y = rmsnorm(x, p["final_norm_g"])
    return y, jnp.stack(new_h), jnp.stack(new_rc), jnp.stack(new_cc)


def setup_inputs(seed: int = 0) -> dict:
    key = jax.random.key(seed)
    ks = iter(jax.random.split(key, 48))
    nrm = lambda shape, s: jax.random.normal(next(ks), shape, jnp.float32) * s
    D = D_MODEL
    u = jax.random.uniform(next(ks), (DEPTH, D_RNN), jnp.float32, 0.9, 0.999)
    a0 = u ** (1.0 / RG_C)
    rg_lambda = jnp.log(a0) - jnp.log1p(-a0)
    return {
        "x_prompt": nrm((BATCH, SEQ, D), 1.0),
        "x_sample": nrm((DEC_BATCH, DEC_SEQ, D), 1.0),
        "c_prompt": nrm((BATCH, D), 1.0),
        "c_sample": nrm((DEC_BATCH, D), 1.0),
        "state_rglru_h": nrm((DEPTH, DEC_BATCH, D_RNN), 0.5),
        "state_rglru_conv": nrm((DEPTH, DEC_BATCH, RG_CONV_W - 1, D_RNN), 1.0),
        "state_cconv": nrm((DEPTH, DEC_BATCH, CC_CONV_W - 1, D_CONV), 0.5),
        "ada_w": nrm((DEPTH, D, 6 * D), 0.5 * D ** -0.5),
        "ada_b": nrm((DEPTH, 6 * D), 0.02),
        "norm1_g": 1.0 + nrm((DEPTH, D), 0.02),
        "norm2_g": 1.0 + nrm((DEPTH, D), 0.02),
        "w_in": nrm((DEPTH, D, D_IN), D ** -0.5),
        "rg_conv_w": nrm((DEPTH, RG_CONV_W, D_RNN), RG_CONV_W ** -0.5),
        "rg_conv_b": nrm((DEPTH, D_RNN), 0.02),
        "rg_wa": nrm((DEPTH, N_RNN_HEADS, RNN_BLOCK, RNN_BLOCK), RNN_BLOCK ** -0.5),
        "rg_ba": nrm((DEPTH, D_RNN), 0.02),
        "rg_wx": nrm((DEPTH, N_RNN_HEADS, RNN_BLOCK, RNN_BLOCK), RNN_BLOCK ** -0.5),
        "rg_bx": nrm((DEPTH, D_RNN), 0.02),
        "rg_lambda": rg_lambda,
        "w_branch_a": nrm((DEPTH, D_RNN, D), D_RNN ** -0.5),
        "cc_dw_w": nrm((DEPTH, CC_CONV_W, D_CONV), CC_CONV_W ** -0.5),
        "cc_dw_b": nrm((DEPTH, D_CONV), 0.02),
        "cc_ln_g": 1.0 + nrm((DEPTH, D_CONV), 0.02),
        "cc_ln_b": nrm((DEPTH, D_CONV), 0.02),
        "w_branch_b": nrm((DEPTH, D_CONV, D), D_CONV ** -0.5),
        "b_branch_b": nrm((DEPTH, D), 0.02),
        "w_out": nrm((DEPTH, D, D), D ** -0.5),
        "ffn_w_gate": nrm((N_DENSE, D, D_FF), D ** -0.5),
        "ffn_w_up": nrm((N_DENSE, D, D_FF), D ** -0.5),
        "ffn_w_down": nrm((N_DENSE, D_FF, D), D_FF ** -0.5),
        "moe_router_w": nrm((N_MOE, D, N_EXPERTS), D ** -0.5),
        "moe_router_b": nrm((N_MOE, N_EXPERTS), 0.01),
        "moe_w_gate": nrm((N_MOE, N_EXPERTS, D, D_FF_EXPERT), D ** -0.5),
        "moe_w_up": nrm((N_MOE, N_EXPERTS, D, D_FF_EXPERT), D ** -0.5),
        "moe_w_down": nrm((N_MOE, N_EXPERTS, D_FF_EXPERT, D), D_FF_EXPERT ** -0.5),
        "final_norm_g": 1.0 + nrm((D,), 0.02),
    }


def reference(x_prompt, x_sample, c_prompt, c_sample, state_rglru_h, state_rglru_conv, state_cconv,
              ada_w, ada_b, norm1_g, norm2_g, w_in, rg_conv_w, rg_conv_b, rg_wa, rg_ba, rg_wx, rg_bx,
              rg_lambda, w_branch_a, cc_dw_w, cc_dw_b, cc_ln_g, cc_ln_b, w_branch_b, b_branch_b, w_out,
              ffn_w_gate, ffn_w_up, ffn_w_down, moe_router_w, moe_router_b, moe_w_gate, moe_w_up,
              moe_w_down, final_norm_g):
    p = {"ada_w": ada_w, "ada_b": ada_b, "norm1_g": norm1_g, "norm2_g": norm2_g, "w_in": w_in,
         "rg_conv_w": rg_conv_w, "rg_conv_b": rg_conv_b, "rg_wa": rg_wa, "rg_ba": rg_ba,
         "rg_wx": rg_wx, "rg_bx": rg_bx, "rg_lambda": rg_lambda, "w_branch_a": w_branch_a,
         "cc_dw_w": cc_dw_w, "cc_dw_b": cc_dw_b, "cc_ln_g": cc_ln_g, "cc_ln_b": cc_ln_b,
         "w_branch_b": w_branch_b, "b_branch_b": b_branch_b, "w_out": w_out,
         "ffn_w_gate": ffn_w_gate, "ffn_w_up": ffn_w_up, "ffn_w_down": ffn_w_down,
         "moe_router_w": moe_router_w, "moe_router_b": moe_router_b, "moe_w_gate": moe_w_gate,
         "moe_w_up": moe_w_up, "moe_w_down": moe_w_down, "final_norm_g": final_norm_g}
    B = x_prompt.shape[0]
    dt = x_prompt.dtype
    h0 = jnp.zeros((DEPTH, B, D_RNN), dt)
    rc0 = jnp.zeros((DEPTH, B, RG_CONV_W - 1, D_RNN), dt)
    cc0 = jnp.zeros((DEPTH, B, CC_CONV_W - 1, D_CONV), dt)
    y_prompt, h_p, rc_p, cc_p = trunk(x_prompt, c_prompt, h0, rc0, cc0, p)
    y_sample, h_s, rc_s, cc_s = trunk(x_sample, c_sample, state_rglru_h, state_rglru_conv, state_cconv, p)
    return (y_prompt, y_sample, h_p, rc_p, cc_p, h_s, rc_s, cc_s)
```

```python
import functools

import jax
import jax.numpy as jnp
from jax import lax
from jax.experimental import pallas as pl
from jax.experimental.pallas import tpu as pltpu

D_MODEL = 1024
N_RNN_HEADS = 16
RG_CONV_W = 4
RG_C = 8.0
CC_CONV_W = 31
N_EXPERTS = 8
EPS = 1e-6

LANES = 128
SUBLANES = 8
VMEM_LIMIT_BYTES = 56 * 1024 * 1024

GATE_GROUP = 256
RC_HIST = SUBLANES
CC_HIST = 32
CONV_ROWS = 32

BF16 = jnp.bfloat16
F32 = jnp.float32


def _sigmoid(x):
    return jax.nn.sigmoid(x)


def _silu(x):
    return x * _sigmoid(x)


def _gelu_tanh(x):
    c = 0.7978845608028654
    return 0.5 * x * (1.0 + jnp.tanh(c * (x + 0.044715 * (x * x * x))))


def _softplus(x):
    return jnp.maximum(x, 0.0) + jnp.log1p(jnp.exp(-jnp.abs(x)))


def _rmsnorm(x, g):
    ms = jnp.mean(x * x, axis=-1, keepdims=True)
    return x * lax.rsqrt(ms + EPS) * g


def _dot(a, b):
    return jnp.dot(a, b, preferred_element_type=F32)


def _resident(block_shape, index_map):
    return pl.BlockSpec(block_shape, index_map, pipeline_mode=pl.Buffered(1))


def _ada_kernel(c_ref, w_ref, b_ref, o_ref):
    s = _silu(c_ref[...]).astype(BF16)
    o_ref[...] = _dot(s, w_ref[...].astype(BF16)) + b_ref[...]


def _ada(c_all, ada_w, ada_b):
    depth, d, n = ada_w.shape
    nb = c_all.shape[0]
    tn = 1536
    return pl.pallas_call(
        _ada_kernel,
        grid=(depth, n // tn),
        in_specs=[
            pl.BlockSpec((nb, d), lambda l, j: (0, 0)),
            pl.BlockSpec((None, d, tn), lambda l, j: (l, 0, j)),
            pl.BlockSpec((None, 1, tn), lambda l, j: (l, 0, j)),
        ],
        out_specs=pl.BlockSpec((None, nb, tn), lambda l, j: (l, 0, j)),
        out_shape=jax.ShapeDtypeStruct((depth, nb, n), F32),
        compiler_params=pltpu.CompilerParams(
            dimension_semantics=("arbitrary", "arbitrary"),
            vmem_limit_bytes=VMEM_LIMIT_BYTES),
        name="ada_mod",
    )(c_all, ada_w, ada_b.reshape(depth, 1, n))


def _scan_rows8(a, b):
    t = lax.broadcasted_iota(jnp.int32, a.shape, 0) & (SUBLANES - 1)
    for k in (1, 2, 4):
        a_prev = pltpu.roll(a, k, 0)
        b_prev = pltpu.roll(b, k, 0)
        take = t >= k
        b = jnp.where(take, a * b_prev + b, b)
        a = jnp.where(take, a * a_prev, a)
    return a, b


def _mixer_kernel(x_ref, mod_ref, h0_ref, rc0_ref, cc0_ref,
                  n1g_ref, win_ref, rgw_ref, rgb_ref, gw_ref, ba_ref, bx_ref,
                  lam_ref, wa_ref, ccw_ref, ccb_ref, lng_ref, lnb_ref, wb_ref,
                  bb_ref, wo_ref,
                  xo_ref, ho_ref, rco_ref, cco_ref,
                  hn_s, xbuf, ubuf, ush, xc_s, a_s, b_s, v_s, hcar,
                  *, S, T, n_chunks):
    C = D_MODEL
    R = S * T
    t_idx = pl.program_id(1)

    @pl.when(t_idx == 0)
    def _load_state():
        xbuf[:, RC_HIST - (RG_CONV_W - 1):RC_HIST, :] = rc0_ref[...]
        ubuf[:, CC_HIST - (CC_CONV_W - 1):CC_HIST, :] = cc0_ref[...]
        hcar[...] = h0_ref[...]

    x3 = x_ref[...]
    sh1 = mod_ref[:, 0:1, :]
    sc1 = mod_ref[:, 1:2, :]
    g1 = mod_ref[:, 2:3, :]
    hn = _rmsnorm(x3, n1g_ref[...]) * (1.0 + sc1) + sh1
    hn_s[...] = hn.reshape(R, C).astype(BF16)

    def proj(j):
        return _dot(hn_s[...], win_ref[:, j * C:(j + 1) * C])

    xbuf[:, RC_HIST:RC_HIST + T, :] = proj(0).reshape(S, T, C)
    xc3 = rgb_ref[...] + rgw_ref[0:1, :] * xbuf[:, RC_HIST - 3:RC_HIST - 3 + T, :]
    for k in range(1, RG_CONV_W):
        lo = RC_HIST - 3 + k
        xc3 = xc3 + rgw_ref[k:k + 1, :] * xbuf[:, lo:lo + T, :]
    xc_s[...] = xc3.reshape(R, C)

    for g in range(C // GATE_GROUP):
        cs = slice(g * GATE_GROUP, (g + 1) * GATE_GROUP)
        xg = xc_s[:, cs]
        pre = _dot(xg.astype(BF16), gw_ref[g])
        r = _sigmoid(pre[:, :GATE_GROUP] + ba_ref[:, cs])
        i = _sigmoid(pre[:, GATE_GROUP:] + bx_ref[:, cs])
        log_a = (-RG_C * _softplus(-lam_ref[:, cs])) * r
        a = jnp.exp(log_a)
        mult = jnp.sqrt(jnp.tanh(-log_a) * (a * a + 1.0))
        a_s[:, cs] = a
        b_s[:, cs] = mult * (i * xg)

    a_c, b_c = _scan_rows8(a_s[...], b_s[...])
    if S == 1:
        a_s[...] = a_c
        b_s[...] = b_c

        def carry_step(gi, h_prev):
            r0 = pl.multiple_of(gi * SUBLANES, SUBLANES)
            h = a_s[pl.ds(r0, SUBLANES), :] * h_prev + b_s[pl.ds(r0, SUBLANES), :]
            b_s[pl.ds(r0, SUBLANES), :] = h
            return h[SUBLANES - 1:SUBLANES, :]

        hcar[0] = lax.fori_loop(0, T // SUBLANES, carry_step, hcar[0])
    else:
        h3 = a_c.reshape(S, T, C) * hcar[...] + b_c.reshape(S, T, C)
        b_s[...] = h3.reshape(R, C)
        hcar[...] = h3[:, T - 1:T, :]

    ya_in = (_gelu_tanh(proj(1)) * b_s[...]).astype(BF16)
    ya = _dot(ya_in, wa_ref[...])

    u = proj(2) * _sigmoid(proj(3))
    ubuf[:, CC_HIST:CC_HIST + T, :] = u.reshape(S, T, C)
    first_tap = CC_HIST - (CC_CONV_W - 1)
    if S == 1:
        span = T + CC_HIST - SUBLANES
        for r in range(1, SUBLANES):
            ush[r, 0:span, :] = ubuf[0, r:r + span, :]

        def conv_step(bi, carry):
            base = pl.multiple_of(bi * CONV_ROWS, CONV_ROWS)
            acc = jnp.broadcast_to(ccb_ref[...], (CONV_ROWS, C))
            for k in range(CC_CONV_W):
                q, r = divmod(first_tap + k, SUBLANES)
                start = base + q * SUBLANES
                if r == 0:
                    win = ubuf[0, pl.ds(start, CONV_ROWS), :]
                else:
                    win = ush[r, pl.ds(start, CONV_ROWS), :]
                acc = acc + ccw_ref[k:k + 1, :] * win
            v_s[pl.ds(base, CONV_ROWS), :] = acc
            return carry

        lax.fori_loop(0, T // CONV_ROWS, conv_step, 0)
    else:
        acc3 = ccb_ref[...] + ccw_ref[0:1, :] * ubuf[:, first_tap:first_tap + T, :]
        for k in range(1, CC_CONV_W):
            acc3 = acc3 + ccw_ref[k:k + 1, :] * ubuf[:, first_tap + k:first_tap + k + T, :]
        v_s[...] = acc3.reshape(R, C)

    v = v_s[...]
    mu = jnp.mean(v, axis=-1, keepdims=True)
    vc = v - mu
    vn = vc * lax.rsqrt(jnp.mean(vc * vc, axis=-1, keepdims=True) + EPS)
    vn = _silu(vn * lng_ref[...] + lnb_ref[...])
    yb = _dot(vn.astype(BF16), wb_ref[...]) + bb_ref[...]

    m = _sigmoid(proj(4)) * ya + _sigmoid(proj(5)) * yb
    o = _dot(m.astype(BF16), wo_ref[...])
    xo_ref[...] = x3 + (1.0 + g1) * o.reshape(S, T, C)

    @pl.when(t_idx == n_chunks - 1)
    def _write_state():
        ho_ref[...] = hcar[...]
        rco_ref[...] = xbuf[:, RC_HIST + T - (RG_CONV_W - 1):RC_HIST + T, :]
        cco_ref[...] = ubuf[:, CC_HIST + T - (CC_CONV_W - 1):CC_HIST + T, :]

    if n_chunks > 1:
        xbuf[:, 0:RC_HIST, :] = xbuf[:, T:T + RC_HIST, :]
        ubuf[:, 0:CC_HIST, :] = ubuf[:, T:T + CC_HIST, :]


def _mixer(l, x, mod, h0, rc0, cc0, w, *, S, T):
    B, Tt, D = x.shape
    C = D
    n_chunks = Tt // T
    assert B % S == 0 and Tt % T == 0 and T % SUBLANES == 0
    assert S == 1 or n_chunks == 1
    if S == 1:
        assert T % CONV_ROWS == 0 and T >= CC_HIST
    R = S * T

    def seq(*tail):
        return pl.BlockSpec((S,) + tail, lambda b, t: (b,) + (0,) * len(tail))

    def layer(*tail):
        return _resident((None,) + tail, lambda b, t: (l,) + (0,) * len(tail))

    in_specs = [
        pl.BlockSpec((S, T, D), lambda b, t: (b, t, 0)),
        seq(6, D),
        seq(1, C), seq(RG_CONV_W - 1, C), seq(CC_CONV_W - 1, C),
        layer(1, D),
        layer(D, 6 * C),
        layer(RG_CONV_W, C), layer(1, C),
        layer(C // GATE_GROUP, GATE_GROUP, 2 * GATE_GROUP),
        layer(1, C), layer(1, C), layer(1, C),
        layer(C, D),
        layer(CC_CONV_W, C), layer(1, C), layer(1, C), layer(1, C),
        layer(C, D), layer(1, D),
        layer(D, D),
    ]
    out_specs = [
        pl.BlockSpec((S, T, D), lambda b, t: (b, t, 0)),
        seq(1, C), seq(RG_CONV_W - 1, C), seq(CC_CONV_W - 1, C),
    ]
    out_shape = [
        jax.ShapeDtypeStruct((B, Tt, D), F32),
        jax.ShapeDtypeStruct((B, 1, C), F32),
        jax.ShapeDtypeStruct((B, RG_CONV_W - 1, C), F32),
        jax.ShapeDtypeStruct((B, CC_CONV_W - 1, C), F32),
    ]
    scratch = [
        pltpu.VMEM((R, D), BF16),
        pltpu.VMEM((S, RC_HIST + T, C), F32),
        pltpu.VMEM((S, CC_HIST + T, C), F32),
        pltpu.VMEM((SUBLANES, CC_HIST + T, C) if S == 1 else (1, SUBLANES, LANES), F32),
        pltpu.VMEM((R, C), F32),
        pltpu.VMEM((R, C), F32),
        pltpu.VMEM((R, C), F32),
        pltpu.VMEM((R, C), F32),
        pltpu.VMEM((S, 1, C), F32),
    ]
    return pl.pallas_call(
        functools.partial(_mixer_kernel, S=S, T=T, n_chunks=n_chunks),
        grid=(B // S, n_chunks),
        in_specs=in_specs,
        out_specs=out_specs,
        out_shape=out_shape,
        scratch_shapes=scratch,
        compiler_params=pltpu.CompilerParams(
            dimension_semantics=("arbitrary", "arbitrary"),
            vmem_limit_bytes=VMEM_LIMIT_BYTES),
        name=f"mixer_S{S}",
    )(x, mod, h0, rc0, cc0,
      w["norm1_g"], w["w_in"], w["rg_conv_w"], w["rg_conv_b"], w["gate_w"],
      w["rg_ba"], w["rg_bx"], w["rg_lambda"], w["w_branch_a"],
      w["cc_dw_w"], w["cc_dw_b"], w["cc_ln_g"], w["cc_ln_b"], w["w_branch_b"],
      w["b_branch_b"], w["w_out"])


def _finish(x3, g2, f, fg_ref, o_ref):
    y = x3 + (1.0 + g2) * f.reshape(x3.shape)
    if fg_ref is not None:
        y = _rmsnorm(y, fg_ref[...])
    o_ref[...] = y


def _ffn_kernel(x_ref, mod_ref, n2g_ref, wg_ref, wu_ref, wd_ref, *rest, n_split, final):
    fg_ref = rest[0] if final else None
    o_ref = rest[-1]
    x3 = x_ref[...]
    S, T, D = x3.shape
    sh2 = mod_ref[:, 3:4, :]
    sc2 = mod_ref[:, 4:5, :]
    g2 = mod_ref[:, 5:6, :]
    hn = (_rmsnorm(x3, n2g_ref[...]) * (1.0 + sc2) + sh2).reshape(S * T, D).astype(BF16)
    ff = wg_ref.shape[1] // n_split
    f = None
    for j in range(n_split):
        cs = slice(j * ff, (j + 1) * ff)
        h = (_silu(_dot(hn, wg_ref[:, cs])) * _dot(hn, wu_ref[:, cs])).astype(BF16)
        part = _dot(h, wd_ref[cs, :])
        f = part if f is None else f + part
    _finish(x3, g2, f, fg_ref, o_ref)


def _ffn(l, j, x, mod, w, final_g, *, S, T):
    B, Tt, D = x.shape
    ff = w["ffn_w_gate"].shape[-1]
    final = final_g is not None
    in_specs = [
        pl.BlockSpec((S, T, D), lambda b, t: (b, t, 0)),
        pl.BlockSpec((S, 6, D), lambda b, t: (b, 0, 0)),
        _resident((None, 1, D), lambda b, t: (l, 0, 0)),
        _resident((None, D, ff), lambda b, t: (j, 0, 0)),
        _resident((None, D, ff), lambda b, t: (j, 0, 0)),
        _resident((None, ff, D), lambda b, t: (j, 0, 0)),
    ]
    args = [x, mod, w["norm2_g"], w["ffn_w_gate"], w["ffn_w_up"], w["ffn_w_down"]]
    if final:
        in_specs.append(_resident((1, D), lambda b, t: (0, 0)))
        args.append(final_g)
    return pl.pallas_call(
        functools.partial(_ffn_kernel, n_split=2, final=final),
        grid=(B // S, Tt // T),
        in_specs=in_specs,
        out_specs=pl.BlockSpec((S, T, D), lambda b, t: (b, t, 0)),
        out_shape=jax.ShapeDtypeStruct((B, Tt, D), F32),
        compiler_params=pltpu.CompilerParams(
            dimension_semantics=("arbitrary", "arbitrary"),
            vmem_limit_bytes=VMEM_LIMIT_BYTES),
        name=f"ffn_S{S}",
    )(*args)


def _moe_kernel(x_ref, mod_ref, n2g_ref, wr_ref, br_ref, wg_ref, wu_ref, wd_ref, *rest, final):
    fg_ref = rest[0] if final else None
    o_ref, hn_s, comb_s, acc_s = rest[-4:]
    e = pl.program_id(2)
    x3 = x_ref[...]
    S, T, D = x3.shape

    @pl.when(e == 0)
    def _route():
        sh2 = mod_ref[:, 3:4, :]
        sc2 = mod_ref[:, 4:5, :]
        hn = (_rmsnorm(x3, n2g_ref[...]) * (1.0 + sc2) + sh2).reshape(S * T, D)
        hn_s[...] = hn.astype(BF16)
        logits = jnp.dot(hn, wr_ref[...], preferred_element_type=F32,
                         precision=lax.Precision.HIGHEST) + br_ref[...]
        z = logits - jnp.max(logits, axis=-1, keepdims=True)
        p = jnp.exp(z)
        p = p / jnp.sum(p, axis=-1, keepdims=True)
        lane = lax.broadcasted_iota(jnp.int32, p.shape, 1).astype(F32)
        far = jnp.float32(p.shape[1])
        m1 = jnp.max(p, axis=-1, keepdims=True)
        i1 = jnp.min(jnp.where(p == m1, lane, far), axis=-1, keepdims=True)
        pick1 = lane == i1
        p2 = jnp.where(pick1, -1.0, p)
        m2 = jnp.max(p2, axis=-1, keepdims=True)
        i2 = jnp.min(jnp.where(p2 == m2, lane, far), axis=-1, keepdims=True)
        pick2 = lane == i2
        comb_s[...] = (jnp.where(pick1, m1, 0.0) + jnp.where(pick2, m2, 0.0)) / (m1 + m2)
        acc_s[...] = jnp.zeros_like(acc_s)

    hn = hn_s[...]
    h = (_silu(_dot(hn, wg_ref[...])) * _dot(hn, wu_ref[...])).astype(BF16)
    ye = _dot(h, wd_ref[...])
    lane = lax.broadcasted_iota(jnp.int32, comb_s.shape, 1)
    ce = jnp.sum(jnp.where(lane == e, comb_s[...], 0.0), axis=-1, keepdims=True)
    acc_s[...] += ce * ye

    @pl.when(e == N_EXPERTS - 1)
    def _done():
        _finish(x3, mod_ref[:, 5:6, :], acc_s[...], fg_ref, o_ref)


def _moe(l, j, x, mod, w, final_g, *, S, T):
    B, Tt, D = x.shape
    ff = w["moe_w_gate"].shape[-1]
    final = final_g is not None
    R = S * T
    in_specs = [
        pl.BlockSpec((S, T, D), lambda b, t, e: (b, t, 0)),
        pl.BlockSpec((S, 6, D), lambda b, t, e: (b, 0, 0)),
        _resident((None, 1, D), lambda b, t, e: (l, 0, 0)),
        _resident((None, D, LANES), lambda b, t, e: (j, 0, 0)),
        _resident((None, 1, LANES), lambda b, t, e: (j, 0, 0)),
        pl.BlockSpec((None, None, D, ff), lambda b, t, e: (j, e, 0, 0)),
        pl.BlockSpec((None, None, D, ff), lambda b, t, e: (j, e, 0, 0)),
        pl.BlockSpec((None, None, ff, D), lambda b, t, e: (j, e, 0, 0)),
    ]
    args = [x, mod, w["norm2_g"], w["router_w"], w["router_b"],
            w["moe_w_gate"], w["moe_w_up"], w["moe_w_down"]]
    if final:
        in_specs.append(_resident((1, D), lambda b, t, e: (0, 0)))
        args.append(final_g)
    return pl.pallas_call(
        functools.partial(_moe_kernel, final=final),
        grid=(B // S, Tt // T, N_EXPERTS),
        in_specs=in_specs,
        out_specs=pl.BlockSpec((S, T, D), lambda b, t, e: (b, t, 0)),
        out_shape=jax.ShapeDtypeStruct((B, Tt, D), F32),
        scratch_shapes=[
            pltpu.VMEM((R, D), BF16),
            pltpu.VMEM((R, LANES), F32),
            pltpu.VMEM((R, D), F32),
        ],
        compiler_params=pltpu.CompilerParams(
            dimension_semantics=("arbitrary", "arbitrary", "arbitrary"),
            vmem_limit_bytes=VMEM_LIMIT_BYTES),
        name=f"moe_S{S}",
    )(*args)


def _block_diag_gate_weights(wa, wx):
    def bd(w):
        L, H, bs, _ = w.shape
        hg = GATE_GROUP // bs
        w5 = w.reshape(L, H // hg, hg, bs, bs)
        eye = jnp.eye(hg, dtype=w.dtype)
        m = jnp.einsum("lqhij,hk->lqhikj", w5, eye)
        return m.reshape(L, H // hg, GATE_GROUP, GATE_GROUP)
    return jnp.concatenate([bd(wa), bd(wx)], axis=-1)


def kernel(x_prompt, x_sample, c_prompt, c_sample, state_rglru_h, state_rglru_conv, state_cconv,
           ada_w, ada_b, norm1_g, norm2_g, w_in, rg_conv_w, rg_conv_b, rg_wa, rg_ba, rg_wx, rg_bx,
           rg_lambda, w_branch_a, cc_dw_w, cc_dw_b, cc_ln_g, cc_ln_b, w_branch_b, b_branch_b, w_out,
           ffn_w_gate, ffn_w_up, ffn_w_down, moe_router_w, moe_router_b, moe_w_gate, moe_w_up,
           moe_w_down, final_norm_g):
    depth = ada_w.shape[0]
    D = D_MODEL
    row = lambda a: a.reshape(a.shape[0], 1, a.shape[-1])
    n_moe = moe_router_w.shape[0]
    router_w = jnp.zeros((n_moe, D, LANES), F32).at[:, :, :N_EXPERTS].set(moe_router_w)
    router_b = jnp.full((n_moe, 1, LANES), -1e30, F32).at[:, 0, :N_EXPERTS].set(moe_router_b)
    w = {
        "norm1_g": row(norm1_g), "norm2_g": row(norm2_g),
        "w_in": w_in.astype(BF16),
        "rg_conv_w": rg_conv_w, "rg_conv_b": row(rg_conv_b),
        "gate_w": _block_diag_gate_weights(rg_wa, rg_wx).astype(BF16),
        "rg_ba": row(rg_ba), "rg_bx": row(rg_bx), "rg_lambda": row(rg_lambda),
        "w_branch_a": w_branch_a.astype(BF16),
        "cc_dw_w": cc_dw_w, "cc_dw_b": row(cc_dw_b),
        "cc_ln_g": row(cc_ln_g), "cc_ln_b": row(cc_ln_b),
        "w_branch_b": w_branch_b.astype(BF16), "b_branch_b": row(b_branch_b),
        "w_out": w_out.astype(BF16),
        "ffn_w_gate": ffn_w_gate.astype(BF16), "ffn_w_up": ffn_w_up.astype(BF16),
        "ffn_w_down": ffn_w_down.astype(BF16),
        "router_w": router_w, "router_b": router_b,
        "moe_w_gate": moe_w_gate.astype(BF16), "moe_w_up": moe_w_up.astype(BF16),
        "moe_w_down": moe_w_down.astype(BF16),
    }
    final_g = final_norm_g.reshape(1, D)

    nb_p = x_prompt.shape[0]
    nb_s = x_sample.shape[0]
    mod_all = _ada(jnp.concatenate([c_prompt, c_sample], axis=0), ada_w, ada_b)
    mod_all = mod_all.reshape(depth, nb_p + nb_s, 6, D)

    dt = x_prompt.dtype
    zeros_p = (jnp.zeros((depth, nb_p, 1, D), dt),
               jnp.zeros((depth, nb_p, RG_CONV_W - 1, D), dt),
               jnp.zeros((depth, nb_p, CC_CONV_W - 1, D), dt))
    state_s = (state_rglru_h.reshape(depth, nb_s, 1, D), state_rglru_conv, state_cconv)

    def trunk(x, mod, states, *, mix_tile, ffn_tile):
        new_h, new_rc, new_cc = [], [], []
        for l in range(depth):
            x, h, rc, cc = _mixer(l, x, mod[l], states[0][l], states[1][l], states[2][l], w,
                                  S=mix_tile[0], T=mix_tile[1])
            fg = final_g if l == depth - 1 else None
            if l % 2 == 0:
                x = _ffn(l, l // 2, x, mod[l], w, fg, S=ffn_tile[0], T=ffn_tile[1])
            else:
                x = _moe(l, l // 2, x, mod[l], w, fg, S=ffn_tile[0], T=ffn_tile[1])
            new_h.append(h[:, 0, :])
            new_rc.append(rc)
            new_cc.append(cc)
        return x, jnp.stack(new_h), jnp.stack(new_rc), jnp.stack(new_cc)

    y_p, h_p, rc_p, cc_p = trunk(x_prompt, mod_all[:, :nb_p], zeros_p,
                                 mix_tile=(1, 256), ffn_tile=(1, 512))
    y_s, h_s, rc_s, cc_s = trunk(x_sample, mod_all[:, nb_p:], state_s,
                                 mix_tile=(32, 8), ffn_tile=(64, 8))
    return (y_p, y_s, h_p, rc_p, cc_p, h_s, rc_s, cc_s)
```

```python
import functools

import jax
import jax.numpy as jnp
from jax import lax
from jax.experimental import pallas as pl
from jax.experimental.pallas import tpu as pltpu

D_MODEL = 1024
N_RNN_HEADS = 16
RG_CONV_W = 4
RG_C = 8.0
CC_CONV_W = 31
N_EXPERTS = 8
EPS = 1e-6

LANES = 128
SUBLANES = 8
VMEM_LIMIT_BYTES = 56 * 1024 * 1024

GATE_GROUP = 256
RC_HIST = SUBLANES
CC_HIST = 32
CONV_BLOCK = 4

BF16 = jnp.bfloat16
F32 = jnp.float32


def _sigmoid(x):
    return jax.nn.sigmoid(x)


def _silu(x):
    return x * _sigmoid(x)


def _gelu_tanh(x):
    c = 0.7978845608028654
    return 0.5 * x * (1.0 + jnp.tanh(c * (x + 0.044715 * (x * x * x))))


def _softplus(x):
    return jnp.maximum(x, 0.0) + jnp.log1p(jnp.exp(-jnp.abs(x)))


def _rmsnorm(x, g):
    ms = jnp.mean(x * x, axis=-1, keepdims=True)
    return x * lax.rsqrt(ms + EPS) * g


def _dot(a, b):
    return jnp.dot(a, b, preferred_element_type=F32)


def _resident(block_shape, index_map):
    return pl.BlockSpec(block_shape, index_map, pipeline_mode=pl.Buffered(1))


def _ada_kernel(c_ref, w_ref, b_ref, o_ref):
    s = _silu(c_ref[...]).astype(BF16)
    o_ref[...] = _dot(s, w_ref[...].astype(BF16)) + b_ref[...]


def _ada(c_all, ada_w, ada_b):
    depth, d, n = ada_w.shape
    nb = c_all.shape[0]
    tn = 1536
    return pl.pallas_call(
        _ada_kernel,
        grid=(depth, n // tn),
        in_specs=[
            pl.BlockSpec((nb, d), lambda l, j: (0, 0)),
            pl.BlockSpec((None, d, tn), lambda l, j: (l, 0, j)),
            pl.BlockSpec((None, 1, tn), lambda l, j: (l, 0, j)),
        ],
        out_specs=pl.BlockSpec((None, nb, tn), lambda l, j: (l, 0, j)),
        out_shape=jax.ShapeDtypeStruct((depth, nb, n), F32),
        compiler_params=pltpu.CompilerParams(
            dimension_semantics=("arbitrary", "arbitrary"),
            vmem_limit_bytes=VMEM_LIMIT_BYTES),
        name="ada_mod",
    )(c_all, ada_w, ada_b.reshape(depth, 1, n))


LANE_BLOCKS = D_MODEL // LANES


def _rows_to_tile_major(tm_ref, val, lane_block0=0):
    R = val.shape[0]
    for j in range(val.shape[1] // LANES):
        lo = (lane_block0 + j) * SUBLANES
        tm_ref[:, lo:lo + SUBLANES, :] = val[:, j * LANES:(j + 1) * LANES].reshape(R // SUBLANES, SUBLANES, LANES)


def _tile_major_to_rows(tm_ref):
    nb = tm_ref.shape[0]
    return jnp.concatenate(
        [tm_ref[:, j * SUBLANES:(j + 1) * SUBLANES, :].reshape(nb * SUBLANES, LANES) for j in range(LANE_BLOCKS)],
        axis=1)


def _load_token(tm_ref, blk, sub):
    return tm_ref[blk, pl.ds(sub, SUBLANES, stride=SUBLANES), :]


def _store_token(tm_ref, blk, sub, val):
    tm_ref[blk, pl.ds(sub, SUBLANES, stride=SUBLANES), :] = val


def _mixer_kernel(x_ref, mod_ref, h0_ref, rc0_ref, cc0_ref,
                  n1g_ref, win_ref, rgw_ref, rgb_ref, gw_ref, ba_ref, bx_ref,
                  lam_ref, wa_ref, ccw_ref, ccb_ref, lng_ref, lnb_ref, wb_ref,
                  bb_ref, wo_ref,
                  xo_ref, ho_ref, rco_ref, cco_ref,
                  hn_s, xbuf, xc_s, a_tm, b_tm, u_tm, y_tm, vt, vtb, hcar,
                  *, S, T, n_chunks):
    C = D_MODEL
    R = S * T
    HT = CC_HIST + T
    t_idx = pl.program_id(1)

    @pl.when(t_idx == 0)
    def _load_state():
        xbuf[:, RC_HIST - (RG_CONV_W - 1):RC_HIST, :] = rc0_ref[...]
        for s in range(S):
            vt[s * HT + CC_HIST - (CC_CONV_W - 1):s * HT + CC_HIST] = cc0_ref[s]
        hcar[...] = h0_ref[...]

    x3 = x_ref[...]
    sh1 = mod_ref[:, 0:1, :]
    sc1 = mod_ref[:, 1:2, :]
    g1 = mod_ref[:, 2:3, :]
    hn = _rmsnorm(x3, n1g_ref[...]) * (1.0 + sc1) + sh1
    hn_s[...] = hn.reshape(R, C).astype(BF16)

    def proj(j):
        return _dot(hn_s[...], win_ref[:, j * C:(j + 1) * C])

    xbuf[:, RC_HIST:RC_HIST + T, :] = proj(0).reshape(S, T, C)
    xc3 = rgb_ref[...] + rgw_ref[0:1, :] * xbuf[:, RC_HIST - 3:RC_HIST - 3 + T, :]
    for k in range(1, RG_CONV_W):
        lo = RC_HIST - 3 + k
        xc3 = xc3 + rgw_ref[k:k + 1, :] * xbuf[:, lo:lo + T, :]
    xc_s[...] = xc3.reshape(R, C)

    for g in range(C // GATE_GROUP):
        cs = slice(g * GATE_GROUP, (g + 1) * GATE_GROUP)
        xg = xc_s[:, cs]
        pre = _dot(xg.astype(BF16), gw_ref[g])
        r = _sigmoid(pre[:, :GATE_GROUP] + ba_ref[:, cs])
        i = _sigmoid(pre[:, GATE_GROUP:] + bx_ref[:, cs])
        log_a = (-RG_C * _softplus(-lam_ref[:, cs])) * r
        a = jnp.exp(log_a)
        mult = jnp.sqrt(jnp.tanh(-log_a) * (a * a + 1.0))
        _rows_to_tile_major(a_tm, a, g * (GATE_GROUP // LANES))
        _rows_to_tile_major(b_tm, mult * (i * xg), g * (GATE_GROUP // LANES))

    for s in range(S):
        h = hcar[s]
        for t in range(T):
            blk, sub = divmod(s * T + t, SUBLANES)
            h = _load_token(a_tm, blk, sub) * h + _load_token(b_tm, blk, sub)
            _store_token(b_tm, blk, sub, h)
        hcar[s] = h

    ya_in = (_gelu_tanh(proj(1)) * _tile_major_to_rows(b_tm)).astype(BF16)
    ya = _dot(ya_in, wa_ref[...])

    _rows_to_tile_major(u_tm, proj(2) * _sigmoid(proj(3)))
    for s in range(S):
        for t in range(T):
            blk, sub = divmod(s * T + t, SUBLANES)
            vt[s * HT + CC_HIST + t] = _load_token(u_tm, blk, sub)

    n_streams, Lp = (1, T // 2) if S == 1 else (S // 2, T)
    hi_off = T // 2 if S == 1 else (S // 2) * HT
    HP = CC_HIST + Lp
    first_tap = CC_HIST - (CC_CONV_W - 1)
    for q in range(n_streams):
        for p in range(first_tap, HP):
            lo = q * HT + p
            vtb[q * HP + p] = jnp.concatenate([vt[lo], vt[lo + hi_off]], axis=0).astype(BF16)

    taps = [ccw_ref[k] for k in range(CC_CONV_W)]
    bias = ccb_ref[...]
    blocks_per_stream = Lp // SUBLANES

    def conv_rows(bi, carry):
        q = 0 if n_streams == 1 else bi // blocks_per_stream
        base = bi * SUBLANES + q * CC_HIST + first_tap
        for i0 in range(0, SUBLANES, CONV_BLOCK):
            acc = [None] * CONV_BLOCK
            for j in range(CONV_BLOCK + CC_CONV_W - 1):
                xj = vtb[base + i0 + j].astype(F32)
                for i in range(CONV_BLOCK):
                    k = j - i
                    if 0 <= k < CC_CONV_W:
                        prod = xj * taps[k].astype(F32)
                        acc[i] = prod if acc[i] is None else acc[i] + prod
            for i in range(CONV_BLOCK):
                _store_token(y_tm, bi, i0 + i, acc[i][:SUBLANES] + bias)
                _store_token(y_tm, bi + R // (2 * SUBLANES), i0 + i, acc[i][SUBLANES:] + bias)
        return carry

    lax.fori_loop(0, R // (2 * SUBLANES), conv_rows, 0)

    v = _tile_major_to_rows(y_tm)
    mu = jnp.mean(v, axis=-1, keepdims=True)
    vc = v - mu
    vn = vc * lax.rsqrt(jnp.mean(vc * vc, axis=-1, keepdims=True) + EPS)
    vn = _silu(vn * lng_ref[...] + lnb_ref[...])
    yb = _dot(vn.astype(BF16), wb_ref[...]) + bb_ref[...]

    m = _sigmoid(proj(4)) * ya + _sigmoid(proj(5)) * yb
    o = _dot(m.astype(BF16), wo_ref[...])
    xo_ref[...] = x3 + (1.0 + g1) * o.reshape(S, T, C)

    @pl.when(t_idx == n_chunks - 1)
    def _write_state():
        ho_ref[...] = hcar[...]
        rco_ref[...] = xbuf[:, RC_HIST + T - (RG_CONV_W - 1):RC_HIST + T, :]
        for s in range(S):
            cco_ref[s] = vt[s * HT + HT - (CC_CONV_W - 1):s * HT + HT]

    if n_chunks > 1:
        xbuf[:, 0:RC_HIST, :] = xbuf[:, T:T + RC_HIST, :]
        for s in range(S):
            vt[s * HT:s * HT + CC_HIST] = vt[s * HT + T:s * HT + T + CC_HIST]


def _mixer(l, x, mod, h0, rc0, cc0, w, *, S, T):
    B, Tt, D = x.shape
    C = D
    n_chunks = Tt // T
    assert B % S == 0 and Tt % T == 0 and T % SUBLANES == 0
    assert S == 1 or n_chunks == 1
    assert n_chunks == 1 or T >= CC_HIST
    R = S * T

    def seq(*tail):
        return pl.BlockSpec((S,) + tail, lambda b, t: (b,) + (0,) * len(tail))

    def layer(*tail):
        return _resident((None,) + tail, lambda b, t: (l,) + (0,) * len(tail))

    in_specs = [
        pl.BlockSpec((S, T, D), lambda b, t: (b, t, 0)),
        seq(6, D),
        seq(SUBLANES, LANES), seq(RG_CONV_W - 1, C), seq(CC_CONV_W - 1, SUBLANES, LANES),
        layer(1, D),
        layer(D, 6 * C),
        layer(RG_CONV_W, C), layer(1, C),
        layer(C // GATE_GROUP, GATE_GROUP, 2 * GATE_GROUP),
        layer(1, C), layer(1, C), layer(1, C),
        layer(C, D),
        layer(CC_CONV_W, 2 * SUBLANES, LANES), layer(SUBLANES, LANES),
        layer(1, C), layer(1, C),
        layer(C, D), layer(1, D),
        layer(D, D),
    ]
    out_specs = [
        pl.BlockSpec((S, T, D), lambda b, t: (b, t, 0)),
        seq(SUBLANES, LANES), seq(RG_CONV_W - 1, C), seq(CC_CONV_W - 1, SUBLANES, LANES),
    ]
    out_shape = [
        jax.ShapeDtypeStruct((B, Tt, D), F32),
        jax.ShapeDtypeStruct((B, SUBLANES, LANES), F32),
        jax.ShapeDtypeStruct((B, RG_CONV_W - 1, C), F32),
        jax.ShapeDtypeStruct((B, CC_CONV_W - 1, SUBLANES, LANES), F32),
    ]
    tile_major = pltpu.VMEM((R // SUBLANES, LANE_BLOCKS * SUBLANES, LANES), F32)
    scratch = [
        pltpu.VMEM((R, D), BF16),
        pltpu.VMEM((S, RC_HIST + T, C), F32),
        pltpu.VMEM((R, C), F32),
        tile_major, tile_major, tile_major, tile_major,
        pltpu.VMEM((S * (CC_HIST + T), SUBLANES, LANES), F32),
        pltpu.VMEM((R // 2 + CC_HIST * (1 if S == 1 else S // 2), 2 * SUBLANES, LANES), BF16),
        pltpu.VMEM((S, SUBLANES, LANES), F32),
    ]
    return pl.pallas_call(
        functools.partial(_mixer_kernel, S=S, T=T, n_chunks=n_chunks),
        grid=(B // S, n_chunks),
        in_specs=in_specs,
        out_specs=out_specs,
        out_shape=out_shape,
        scratch_shapes=scratch,
        compiler_params=pltpu.CompilerParams(
            dimension_semantics=("arbitrary", "arbitrary"),
            vmem_limit_bytes=VMEM_LIMIT_BYTES),
        name=f"mixer_S{S}",
    )(x, mod, h0, rc0, cc0,
      w["norm1_g"], w["w_in"], w["rg_conv_w"], w["rg_conv_b"], w["gate_w"],
      w["rg_ba"], w["rg_bx"], w["rg_lambda"], w["w_branch_a"],
      w["cc_dw_w"], w["cc_dw_b"], w["cc_ln_g"], w["cc_ln_b"], w["w_branch_b"],
      w["b_branch_b"], w["w_out"])


def _finish(x3, g2, f, fg_ref, o_ref):
    y = x3 + (1.0 + g2) * f.reshape(x3.shape)
    if fg_ref is not None:
        y = _rmsnorm(y, fg_ref[...])
    o_ref[...] = y


def _ffn_kernel(x_ref, mod_ref, n2g_ref, wg_ref, wu_ref, wd_ref, *rest, n_split, final):
    fg_ref = rest[0] if final else None
    o_ref = rest[-1]
    x3 = x_ref[...]
    S, T, D = x3.shape
    sh2 = mod_ref[:, 3:4, :]
    sc2 = mod_ref[:, 4:5, :]
    g2 = mod_ref[:, 5:6, :]
    hn = (_rmsnorm(x3, n2g_ref[...]) * (1.0 + sc2) + sh2).reshape(S * T, D).astype(BF16)
    ff = wg_ref.shape[1] // n_split
    f = None
    for j in range(n_split):
        cs = slice(j * ff, (j + 1) * ff)
        h = (_silu(_dot(hn, wg_ref[:, cs])) * _dot(hn, wu_ref[:, cs])).astype(BF16)
        part = _dot(h, wd_ref[cs, :])
        f = part if f is None else f + part
    _finish(x3, g2, f, fg_ref, o_ref)


def _ffn(l, j, x, mod, w, final_g, *, S, T):
    B, Tt, D = x.shape
    ff = w["ffn_w_gate"].shape[-1]
    final = final_g is not None
    in_specs = [
        pl.BlockSpec((S, T, D), lambda b, t: (b, t, 0)),
        pl.BlockSpec((S, 6, D), lambda b, t: (b, 0, 0)),
        _resident((None, 1, D), lambda b, t: (l, 0, 0)),
        _resident((None, D, ff), lambda b, t: (j, 0, 0)),
        _resident((None, D, ff), lambda b, t: (j, 0, 0)),
        _resident((None, ff, D), lambda b, t: (j, 0, 0)),
    ]
    args = [x, mod, w["norm2_g"], w["ffn_w_gate"], w["ffn_w_up"], w["ffn_w_down"]]
    if final:
        in_specs.append(_resident((1, D), lambda b, t: (0, 0)))
        args.append(final_g)
    return pl.pallas_call(
        functools.partial(_ffn_kernel, n_split=2, final=final),
        grid=(B // S, Tt // T),
        in_specs=in_specs,
        out_specs=pl.BlockSpec((S, T, D), lambda b, t: (b, t, 0)),
        out_shape=jax.ShapeDtypeStruct((B, Tt, D), F32),
        compiler_params=pltpu.CompilerParams(
            dimension_semantics=("arbitrary", "arbitrary"),
            vmem_limit_bytes=VMEM_LIMIT_BYTES),
        name=f"ffn_S{S}",
    )(*args)


def _moe_kernel(x_ref, mod_ref, n2g_ref, wr_ref, br_ref, wg_ref, wu_ref, wd_ref, *rest, final):
    fg_ref = rest[0] if final else None
    o_ref, hn_s, comb_s, acc_s = rest[-4:]
    e = pl.program_id(2)
    x3 = x_ref[...]
    S, T, D = x3.shape

    @pl.when(e == 0)
    def _route():
        sh2 = mod_ref[:, 3:4, :]
        sc2 = mod_ref[:, 4:5, :]
        hn = (_rmsnorm(x3, n2g_ref[...]) * (1.0 + sc2) + sh2).reshape(S * T, D)
        hn_s[...] = hn.astype(BF16)
        logits = jnp.dot(hn, wr_ref[...], preferred_element_type=F32,
                         precision=lax.Precision.HIGHEST) + br_ref[...]
        z = logits - jnp.max(logits, axis=-1, keepdims=True)
        p = jnp.exp(z)
        p = p / jnp.sum(p, axis=-1, keepdims=True)
        lane = lax.broadcasted_iota(jnp.int32, p.shape, 1).astype(F32)
        far = jnp.float32(p.shape[1])
        m1 = jnp.max(p, axis=-1, keepdims=True)
        i1 = jnp.min(jnp.where(p == m1, lane, far), axis=-1, keepdims=True)
        pick1 = lane == i1
        p2 = jnp.where(pick1, -1.0, p)
        m2 = jnp.max(p2, axis=-1, keepdims=True)
        i2 = jnp.min(jnp.where(p2 == m2, lane, far), axis=-1, keepdims=True)
        pick2 = lane == i2
        comb_s[...] = (jnp.where(pick1, m1, 0.0) + jnp.where(pick2, m2, 0.0)) / (m1 + m2)
        acc_s[...] = jnp.zeros_like(acc_s)

    hn = hn_s[...]
    h = (_silu(_dot(hn, wg_ref[...])) * _dot(hn, wu_ref[...])).astype(BF16)
    ye = _dot(h, wd_ref[...])
    lane = lax.broadcasted_iota(jnp.int32, comb_s.shape, 1)
    ce = jnp.sum(jnp.where(lane == e, comb_s[...], 0.0), axis=-1, keepdims=True)
    acc_s[...] += ce * ye

    @pl.when(e == N_EXPERTS - 1)
    def _done():
        _finish(x3, mod_ref[:, 5:6, :], acc_s[...], fg_ref, o_ref)


def _moe(l, j, x, mod, w, final_g, *, S, T):
    B, Tt, D = x.shape
    ff = w["moe_w_gate"].shape[-1]
    final = final_g is not None
    R = S * T
    in_specs = [
        pl.BlockSpec((S, T, D), lambda b, t, e: (b, t, 0)),
        pl.BlockSpec((S, 6, D), lambda b, t, e: (b, 0, 0)),
        _resident((None, 1, D), lambda b, t, e: (l, 0, 0)),
        _resident((None, D, LANES), lambda b, t, e: (j, 0, 0)),
        _resident((None, 1, LANES), lambda b, t, e: (j, 0, 0)),
        pl.BlockSpec((None, None, D, ff), lambda b, t, e: (j, e, 0, 0)),
        pl.BlockSpec((None, None, D, ff), lambda b, t, e: (j, e, 0, 0)),
        pl.BlockSpec((None, None, ff, D), lambda b, t, e: (j, e, 0, 0)),
    ]
    args = [x, mod, w["norm2_g"], w["router_w"], w["router_b"],
            w["moe_w_gate"], w["moe_w_up"], w["moe_w_down"]]
    if final:
        in_specs.append(_resident((1, D), lambda b, t, e: (0, 0)))
        args.append(final_g)
    return pl.pallas_call(
        functools.partial(_moe_kernel, final=final),
        grid=(B // S, Tt // T, N_EXPERTS),
        in_specs=in_specs,
        out_specs=pl.BlockSpec((S, T, D), lambda b, t, e: (b, t, 0)),
        out_shape=jax.ShapeDtypeStruct((B, Tt, D), F32),
        scratch_shapes=[
            pltpu.VMEM((R, D), BF16),
            pltpu.VMEM((R, LANES), F32),
            pltpu.VMEM((R, D), F32),
        ],
        compiler_params=pltpu.CompilerParams(
            dimension_semantics=("arbitrary", "arbitrary", "arbitrary"),
            vmem_limit_bytes=VMEM_LIMIT_BYTES),
        name=f"moe_S{S}",
    )(*args)


def _block_diag_gate_weights(wa, wx):
    def bd(w):
        L, H, bs, _ = w.shape
        hg = GATE_GROUP // bs
        w5 = w.reshape(L, H // hg, hg, bs, bs)
        eye = jnp.eye(hg, dtype=w.dtype)
        m = jnp.einsum("lqhij,hk->lqhikj", w5, eye)
        return m.reshape(L, H // hg, GATE_GROUP, GATE_GROUP)
    return jnp.concatenate([bd(wa), bd(wx)], axis=-1)


def kernel(x_prompt, x_sample, c_prompt, c_sample, state_rglru_h, state_rglru_conv, state_cconv,
           ada_w, ada_b, norm1_g, norm2_g, w_in, rg_conv_w, rg_conv_b, rg_wa, rg_ba, rg_wx, rg_bx,
           rg_lambda, w_branch_a, cc_dw_w, cc_dw_b, cc_ln_g, cc_ln_b, w_branch_b, b_branch_b, w_out,
           ffn_w_gate, ffn_w_up, ffn_w_down, moe_router_w, moe_router_b, moe_w_gate, moe_w_up,
           moe_w_down, final_norm_g):
    depth = ada_w.shape[0]
    D = D_MODEL
    row = lambda a: a.reshape(a.shape[0], 1, a.shape[-1])
    n_moe = moe_router_w.shape[0]
    router_w = jnp.zeros((n_moe, D, LANES), F32).at[:, :, :N_EXPERTS].set(moe_router_w)
    router_b = jnp.full((n_moe, 1, LANES), -1e30, F32).at[:, 0, :N_EXPERTS].set(moe_router_b)
    w = {
        "norm1_g": row(norm1_g), "norm2_g": row(norm2_g),
        "w_in": w_in.astype(BF16),
        "rg_conv_w": rg_conv_w, "rg_conv_b": row(rg_conv_b),
        "gate_w": _block_diag_gate_weights(rg_wa, rg_wx).astype(BF16),
        "rg_ba": row(rg_ba), "rg_bx": row(rg_bx), "rg_lambda": row(rg_lambda),
        "w_branch_a": w_branch_a.astype(BF16),
        "cc_dw_w": jnp.tile(cc_dw_w.reshape(depth, CC_CONV_W, SUBLANES, LANES), (1, 1, 2, 1)).astype(BF16),
        "cc_dw_b": cc_dw_b.reshape(depth, SUBLANES, LANES),
        "cc_ln_g": row(cc_ln_g), "cc_ln_b": row(cc_ln_b),
        "w_branch_b": w_branch_b.astype(BF16), "b_branch_b": row(b_branch_b),
        "w_out": w_out.astype(BF16),
        "ffn_w_gate": ffn_w_gate.astype(BF16), "ffn_w_up": ffn_w_up.astype(BF16),
        "ffn_w_down": ffn_w_down.astype(BF16),
        "router_w": router_w, "router_b": router_b,
        "moe_w_gate": moe_w_gate.astype(BF16), "moe_w_up": moe_w_up.astype(BF16),
        "moe_w_down": moe_w_down.astype(BF16),
    }
    final_g = final_norm_g.reshape(1, D)

    nb_p = x_prompt.shape[0]
    nb_s = x_sample.shape[0]
    mod_all = _ada(jnp.concatenate([c_prompt, c_sample], axis=0), ada_w, ada_b)
    mod_all = mod_all.reshape(depth, nb_p + nb_s, 6, D)

    dt = x_prompt.dtype
    zeros_p = (jnp.zeros((depth, nb_p, SUBLANES, LANES), dt),
               jnp.zeros((depth, nb_p, RG_CONV_W - 1, D), dt),
               jnp.zeros((depth, nb_p, CC_CONV_W - 1, SUBLANES, LANES), dt))
    state_s = (state_rglru_h.reshape(depth, nb_s, SUBLANES, LANES), state_rglru_conv,
               state_cconv.reshape(depth, nb_s, CC_CONV_W - 1, SUBLANES, LANES))

    def trunk(x, mod, states, *, mix_tile, ffn_tile):
        new_h, new_rc, new_cc = [], [], []
        for l in range(depth):
            x, h, rc, cc = _mixer(l, x, mod[l], states[0][l], states[1][l], states[2][l], w,
                                  S=mix_tile[0], T=mix_tile[1])
            fg = final_g if l == depth - 1 else None
            if l % 2 == 0:
                x = _ffn(l, l // 2, x, mod[l], w, fg, S=ffn_tile[0], T=ffn_tile[1])
            else:
                x = _moe(l, l // 2, x, mod[l], w, fg, S=ffn_tile[0], T=ffn_tile[1])
            new_h.append(h.reshape(h.shape[0], D))
            new_rc.append(rc)
            new_cc.append(cc.reshape(cc.shape[0], CC_CONV_W - 1, D))
        return x, jnp.stack(new_h), jnp.stack(new_rc), jnp.stack(new_cc)

    y_p, h_p, rc_p, cc_p = trunk(x_prompt, mod_all[:, :nb_p], zeros_p,
                                 mix_tile=(1, 256), ffn_tile=(1, 512))
    y_s, h_s, rc_s, cc_s = trunk(x_sample, mod_all[:, nb_p:], state_s,
                                 mix_tile=(16, 8), ffn_tile=(64, 8))
    return (y_p, y_s, h_p, rc_p, cc_p, h_s, rc_s, cc_s)
```

```python
import functools

import jax
import jax.numpy as jnp
from jax import lax
from jax.experimental import pallas as pl
from jax.experimental.pallas import tpu as pltpu

D_MODEL = 1024
N_RNN_HEADS = 16
RG_CONV_W = 4
RG_C = 8.0
CC_CONV_W = 31
N_EXPERTS = 8
TOP_K = 2
EPS = 1e-6

LANES = 128
SUBLANES = 8
VMEM_LIMIT_BYTES = 56 * 1024 * 1024

GATE_GROUP = 256
RC_HIST = SUBLANES
CC_HIST = 32
CONV_BLOCK = 4

BF16 = jnp.bfloat16
F32 = jnp.float32


def _sigmoid(x):
    return jax.nn.sigmoid(x)


def _silu(x):
    return x * _sigmoid(x)


def _gelu_tanh(x):
    c = 0.7978845608028654
    return 0.5 * x * (1.0 + jnp.tanh(c * (x + 0.044715 * (x * x * x))))


def _softplus(x):
    return jnp.maximum(x, 0.0) + jnp.log1p(jnp.exp(-jnp.abs(x)))


def _rmsnorm(x, g):
    ms = jnp.mean(x * x, axis=-1, keepdims=True)
    return x * lax.rsqrt(ms + EPS) * g


def _dot(a, b):
    return jnp.dot(a, b, preferred_element_type=F32)


def _resident(block_shape, index_map):
    return pl.BlockSpec(block_shape, index_map, pipeline_mode=pl.Buffered(1))


def _ada_kernel(c_ref, w_ref, b_ref, o_ref):
    s = _silu(c_ref[...]).astype(BF16)
    o_ref[...] = _dot(s, w_ref[...].astype(BF16)) + b_ref[...]


def _ada(c_all, ada_w, ada_b):
    depth, d, n = ada_w.shape
    nb = c_all.shape[0]
    tn = 1536
    return pl.pallas_call(
        _ada_kernel,
        grid=(depth, n // tn),
        in_specs=[
            pl.BlockSpec((nb, d), lambda l, j: (0, 0)),
            pl.BlockSpec((None, d, tn), lambda l, j: (l, 0, j)),
            pl.BlockSpec((None, 1, tn), lambda l, j: (l, 0, j)),
        ],
        out_specs=pl.BlockSpec((None, nb, tn), lambda l, j: (l, 0, j)),
        out_shape=jax.ShapeDtypeStruct((depth, nb, n), F32),
        compiler_params=pltpu.CompilerParams(
            dimension_semantics=("arbitrary", "arbitrary"),
            vmem_limit_bytes=VMEM_LIMIT_BYTES),
        name="ada_mod",
    )(c_all, ada_w, ada_b.reshape(depth, 1, n))


LANE_BLOCKS = D_MODEL // LANES


def _rows_to_tile_major(tm_ref, val, lane_block0=0):
    R = val.shape[0]
    for j in range(val.shape[1] // LANES):
        lo = (lane_block0 + j) * SUBLANES
        tm_ref[:, lo:lo + SUBLANES, :] = val[:, j * LANES:(j + 1) * LANES].reshape(R // SUBLANES, SUBLANES, LANES)


def _tile_major_to_rows(tm_ref):
    nb = tm_ref.shape[0]
    return jnp.concatenate(
        [tm_ref[:, j * SUBLANES:(j + 1) * SUBLANES, :].reshape(nb * SUBLANES, LANES) for j in range(LANE_BLOCKS)],
        axis=1)


def _load_token(tm_ref, blk, sub):
    return tm_ref[blk, pl.ds(sub, SUBLANES, stride=SUBLANES), :]


def _store_token(tm_ref, blk, sub, val):
    tm_ref[blk, pl.ds(sub, SUBLANES, stride=SUBLANES), :] = val


def _mixer_kernel(x_ref, mod_ref, h0_ref, rc0_ref, cc0_ref,
                  n1g_ref, win_ref, rgw_ref, rgb_ref, gw_ref, ba_ref, bx_ref,
                  lam_ref, wa_ref, ccw_ref, ccb_ref, lng_ref, lnb_ref, wb_ref,
                  bb_ref, wo_ref,
                  xo_ref, ho_ref, rco_ref, cco_ref,
                  hn_s, xbuf, xc_s, a_tm, b_tm, u_tm, y_tm, vt, vtb, hcar,
                  *, S, T, n_chunks):
    C = D_MODEL
    R = S * T
    HT = CC_HIST + T
    t_idx = pl.program_id(1)

    @pl.when(t_idx == 0)
    def _load_state():
        xbuf[:, RC_HIST - (RG_CONV_W - 1):RC_HIST, :] = rc0_ref[...]
        for s in range(S):
            vt[s * HT + CC_HIST - (CC_CONV_W - 1):s * HT + CC_HIST] = cc0_ref[s]
        hcar[...] = h0_ref[...]

    x3 = x_ref[...]
    sh1 = mod_ref[:, 0:1, :]
    sc1 = mod_ref[:, 1:2, :]
    g1 = mod_ref[:, 2:3, :]
    hn = _rmsnorm(x3, n1g_ref[...]) * (1.0 + sc1) + sh1
    hn_s[...] = hn.reshape(R, C).astype(BF16)

    def proj(j):
        return _dot(hn_s[...], win_ref[:, j * C:(j + 1) * C])

    xbuf[:, RC_HIST:RC_HIST + T, :] = proj(0).reshape(S, T, C)
    xc3 = rgb_ref[...] + rgw_ref[0:1, :] * xbuf[:, RC_HIST - 3:RC_HIST - 3 + T, :]
    for k in range(1, RG_CONV_W):
        lo = RC_HIST - 3 + k
        xc3 = xc3 + rgw_ref[k:k + 1, :] * xbuf[:, lo:lo + T, :]
    xc_s[...] = xc3.reshape(R, C)

    for g in range(C // GATE_GROUP):
        cs = slice(g * GATE_GROUP, (g + 1) * GATE_GROUP)
        xg = xc_s[:, cs]
        pre = _dot(xg.astype(BF16), gw_ref[g])
        r = _sigmoid(pre[:, :GATE_GROUP] + ba_ref[:, cs])
        i = _sigmoid(pre[:, GATE_GROUP:] + bx_ref[:, cs])
        log_a = (-RG_C * _softplus(-lam_ref[:, cs])) * r
        a = jnp.exp(log_a)
        mult = jnp.sqrt(jnp.tanh(-log_a) * (a * a + 1.0))
        _rows_to_tile_major(a_tm, a, g * (GATE_GROUP // LANES))
        _rows_to_tile_major(b_tm, mult * (i * xg), g * (GATE_GROUP // LANES))

    for s in range(S):
        h = hcar[s]
        for t in range(T):
            blk, sub = divmod(s * T + t, SUBLANES)
            h = _load_token(a_tm, blk, sub) * h + _load_token(b_tm, blk, sub)
            _store_token(b_tm, blk, sub, h)
        hcar[s] = h

    ya_in = (_gelu_tanh(proj(1)) * _tile_major_to_rows(b_tm)).astype(BF16)
    ya = _dot(ya_in, wa_ref[...])

    _rows_to_tile_major(u_tm, proj(2) * _sigmoid(proj(3)))
    for s in range(S):
        for t in range(T):
            blk, sub = divmod(s * T + t, SUBLANES)
            vt[s * HT + CC_HIST + t] = _load_token(u_tm, blk, sub)

    n_streams, Lp = (1, T // 2) if S == 1 else (S // 2, T)
    hi_off = T // 2 if S == 1 else (S // 2) * HT
    HP = CC_HIST + Lp
    first_tap = CC_HIST - (CC_CONV_W - 1)
    for q in range(n_streams):
        for p in range(first_tap, HP):
            lo = q * HT + p
            vtb[q * HP + p] = jnp.concatenate([vt[lo], vt[lo + hi_off]], axis=0).astype(BF16)

    taps = [ccw_ref[k] for k in range(CC_CONV_W)]
    bias = ccb_ref[...]
    blocks_per_stream = Lp // SUBLANES

    def conv_rows(bi, carry):
        q = 0 if n_streams == 1 else bi // blocks_per_stream
        base = bi * SUBLANES + q * CC_HIST + first_tap
        for i0 in range(0, SUBLANES, CONV_BLOCK):
            acc = [None] * CONV_BLOCK
            for j in range(CONV_BLOCK + CC_CONV_W - 1):
                xj = vtb[base + i0 + j].astype(F32)
                for i in range(CONV_BLOCK):
                    k = j - i
                    if 0 <= k < CC_CONV_W:
                        prod = xj * taps[k].astype(F32)
                        acc[i] = prod if acc[i] is None else acc[i] + prod
            for i in range(CONV_BLOCK):
                _store_token(y_tm, bi, i0 + i, acc[i][:SUBLANES] + bias)
                _store_token(y_tm, bi + R // (2 * SUBLANES), i0 + i, acc[i][SUBLANES:] + bias)
        return carry

    lax.fori_loop(0, R // (2 * SUBLANES), conv_rows, 0)

    v = _tile_major_to_rows(y_tm)
    mu = jnp.mean(v, axis=-1, keepdims=True)
    vc = v - mu
    vn = vc * lax.rsqrt(jnp.mean(vc * vc, axis=-1, keepdims=True) + EPS)
    vn = _silu(vn * lng_ref[...] + lnb_ref[...])
    yb = _dot(vn.astype(BF16), wb_ref[...]) + bb_ref[...]

    m = _sigmoid(proj(4)) * ya + _sigmoid(proj(5)) * yb
    o = _dot(m.astype(BF16), wo_ref[...])
    xo_ref[...] = x3 + (1.0 + g1) * o.reshape(S, T, C)

    @pl.when(t_idx == n_chunks - 1)
    def _write_state():
        ho_ref[...] = hcar[...]
        rco_ref[...] = xbuf[:, RC_HIST + T - (RG_CONV_W - 1):RC_HIST + T, :]
        for s in range(S):
            cco_ref[s] = vt[s * HT + HT - (CC_CONV_W - 1):s * HT + HT]

    if n_chunks > 1:
        xbuf[:, 0:RC_HIST, :] = xbuf[:, T:T + RC_HIST, :]
        for s in range(S):
            vt[s * HT:s * HT + CC_HIST] = vt[s * HT + T:s * HT + T + CC_HIST]


def _mixer(l, x, mod, h0, rc0, cc0, w, *, S, T):
    B, Tt, D = x.shape
    C = D
    n_chunks = Tt // T
    assert B % S == 0 and Tt % T == 0 and T % SUBLANES == 0
    assert S == 1 or n_chunks == 1
    assert n_chunks == 1 or T >= CC_HIST
    R = S * T

    def seq(*tail):
        return pl.BlockSpec((S,) + tail, lambda b, t: (b,) + (0,) * len(tail))

    def layer(*tail):
        return _resident((None,) + tail, lambda b, t: (l,) + (0,) * len(tail))

    in_specs = [
        pl.BlockSpec((S, T, D), lambda b, t: (b, t, 0)),
        seq(6, D),
        seq(SUBLANES, LANES), seq(RG_CONV_W - 1, C), seq(CC_CONV_W - 1, SUBLANES, LANES),
        layer(1, D),
        layer(D, 6 * C),
        layer(RG_CONV_W, C), layer(1, C),
        layer(C // GATE_GROUP, GATE_GROUP, 2 * GATE_GROUP),
        layer(1, C), layer(1, C), layer(1, C),
        layer(C, D),
        layer(CC_CONV_W, 2 * SUBLANES, LANES), layer(SUBLANES, LANES),
        layer(1, C), layer(1, C),
        layer(C, D), layer(1, D),
        layer(D, D),
    ]
    out_specs = [
        pl.BlockSpec((S, T, D), lambda b, t: (b, t, 0)),
        seq(SUBLANES, LANES), seq(RG_CONV_W - 1, C), seq(CC_CONV_W - 1, SUBLANES, LANES),
    ]
    out_shape = [
        jax.ShapeDtypeStruct((B, Tt, D), F32),
        jax.ShapeDtypeStruct((B, SUBLANES, LANES), F32),
        jax.ShapeDtypeStruct((B, RG_CONV_W - 1, C), F32),
        jax.ShapeDtypeStruct((B, CC_CONV_W - 1, SUBLANES, LANES), F32),
    ]
    tile_major = pltpu.VMEM((R // SUBLANES, LANE_BLOCKS * SUBLANES, LANES), F32)
    scratch = [
        pltpu.VMEM((R, D), BF16),
        pltpu.VMEM((S, RC_HIST + T, C), F32),
        pltpu.VMEM((R, C), F32),
        tile_major, tile_major, tile_major, tile_major,
        pltpu.VMEM((S * (CC_HIST + T), SUBLANES, LANES), F32),
        pltpu.VMEM((R // 2 + CC_HIST * (1 if S == 1 else S // 2), 2 * SUBLANES, LANES), BF16),
        pltpu.VMEM((S, SUBLANES, LANES), F32),
    ]
    return pl.pallas_call(
        functools.partial(_mixer_kernel, S=S, T=T, n_chunks=n_chunks),
        grid=(B // S, n_chunks),
        in_specs=in_specs,
        out_specs=out_specs,
        out_shape=out_shape,
        scratch_shapes=scratch,
        compiler_params=pltpu.CompilerParams(
            dimension_semantics=("arbitrary", "arbitrary"),
            vmem_limit_bytes=VMEM_LIMIT_BYTES),
        name=f"mixer_S{S}",
    )(x, mod, h0, rc0, cc0,
      w["norm1_g"], w["w_in"], w["rg_conv_w"], w["rg_conv_b"], w["gate_w"],
      w["rg_ba"], w["rg_bx"], w["rg_lambda"], w["w_branch_a"],
      w["cc_dw_w"], w["cc_dw_b"], w["cc_ln_g"], w["cc_ln_b"], w["w_branch_b"],
      w["b_branch_b"], w["w_out"])


def _finish(x3, g2, f, fg_ref, o_ref):
    y = x3 + (1.0 + g2) * f.reshape(x3.shape)
    if fg_ref is not None:
        y = _rmsnorm(y, fg_ref[...])
    o_ref[...] = y


def _ffn_kernel(x_ref, mod_ref, n2g_ref, wg_ref, wu_ref, wd_ref, *rest, n_split, final):
    fg_ref = rest[0] if final else None
    o_ref = rest[-1]
    x3 = x_ref[...]
    S, T, D = x3.shape
    sh2 = mod_ref[:, 3:4, :]
    sc2 = mod_ref[:, 4:5, :]
    g2 = mod_ref[:, 5:6, :]
    hn = (_rmsnorm(x3, n2g_ref[...]) * (1.0 + sc2) + sh2).reshape(S * T, D).astype(BF16)
    ff = wg_ref.shape[1] // n_split
    f = None
    for j in range(n_split):
        cs = slice(j * ff, (j + 1) * ff)
        h = (_silu(_dot(hn, wg_ref[:, cs])) * _dot(hn, wu_ref[:, cs])).astype(BF16)
        part = _dot(h, wd_ref[cs, :])
        f = part if f is None else f + part
    _finish(x3, g2, f, fg_ref, o_ref)


def _ffn(l, j, x, mod, w, final_g, *, S, T):
    B, Tt, D = x.shape
    ff = w["ffn_w_gate"].shape[-1]
    final = final_g is not None
    in_specs = [
        pl.BlockSpec((S, T, D), lambda b, t: (b, t, 0)),
        pl.BlockSpec((S, 6, D), lambda b, t: (b, 0, 0)),
        _resident((None, 1, D), lambda b, t: (l, 0, 0)),
        _resident((None, D, ff), lambda b, t: (j, 0, 0)),
        _resident((None, D, ff), lambda b, t: (j, 0, 0)),
        _resident((None, ff, D), lambda b, t: (j, 0, 0)),
    ]
    args = [x, mod, w["norm2_g"], w["ffn_w_gate"], w["ffn_w_up"], w["ffn_w_down"]]
    if final:
        in_specs.append(_resident((1, D), lambda b, t: (0, 0)))
        args.append(final_g)
    return pl.pallas_call(
        functools.partial(_ffn_kernel, n_split=2, final=final),
        grid=(B // S, Tt // T),
        in_specs=in_specs,
        out_specs=pl.BlockSpec((S, T, D), lambda b, t: (b, t, 0)),
        out_shape=jax.ShapeDtypeStruct((B, Tt, D), F32),
        compiler_params=pltpu.CompilerParams(
            dimension_semantics=("arbitrary", "arbitrary"),
            vmem_limit_bytes=VMEM_LIMIT_BYTES),
        name=f"ffn_S{S}",
    )(*args)


MOE_BLOCK = 1024
MOE_ROWS = 128
MOE_SLOTS = TOP_K * MOE_BLOCK + N_EXPERTS * MOE_ROWS


def _route_kernel(x_ref, mod_ref, n2g_ref, wr_ref, br_ref, hn_ref, meta_ref, info_ref):
    x3 = x_ref[...]
    S, T, D = x3.shape
    R = S * T
    hn3 = _rmsnorm(x3, n2g_ref[...]) * (1.0 + mod_ref[:, 4:5, :]) + mod_ref[:, 3:4, :]
    hn_ref[...] = hn3
    hn = hn3.reshape(R, D)
    hn_hi = hn.astype(BF16)
    hn_lo = (hn - hn_hi.astype(F32)).astype(BF16)
    both = _dot(hn_hi, wr_ref[...])
    logits = both[:, :LANES] + both[:, LANES:] + _dot(hn_lo, wr_ref[:, :LANES]) + br_ref[...]
    z = logits - jnp.max(logits, axis=-1, keepdims=True)
    p = jnp.exp(z)
    p = p / jnp.sum(p, axis=-1, keepdims=True)
    lane_i = lax.broadcasted_iota(jnp.int32, p.shape, 1)
    lane = lane_i.astype(F32)
    far = jnp.float32(p.shape[1])
    m1 = jnp.max(p, axis=-1, keepdims=True)
    i1 = jnp.min(jnp.where(p == m1, lane, far), axis=-1, keepdims=True)
    pick1 = lane == i1
    p2 = jnp.where(pick1, -1.0, p)
    m2 = jnp.max(p2, axis=-1, keepdims=True)
    i2 = jnp.min(jnp.where(p2 == m2, lane, far), axis=-1, keepdims=True)
    pick2 = lane == i2
    onehot = jnp.where(pick1, 1.0, 0.0) + jnp.where(pick2, 1.0, 0.0)
    earlier = (lax.broadcasted_iota(jnp.int32, (R, R), 1) < lax.broadcasted_iota(jnp.int32, (R, R), 0))
    rank = _dot(jnp.where(earlier, 1.0, 0.0).astype(BF16), onehot.astype(BF16))
    count = rank[R - 1:R, :] + onehot[R - 1:R, :]
    count8 = jnp.broadcast_to(count, (SUBLANES, LANES))
    padded = jnp.floor((count8 + (MOE_ROWS - 1)) * (1.0 / MOE_ROWS)) * MOE_ROWS
    incl = padded
    lane8 = lax.broadcasted_iota(jnp.int32, (SUBLANES, LANES), 1)
    for k in (1, 2, 4):
        incl = incl + jnp.where(lane8 >= k, pltpu.roll(incl, k, 1), 0.0)
    start8 = incl - padded
    slot = start8[0:1, :] + rank
    slot1 = jnp.sum(jnp.where(pick1, slot, 0.0), axis=-1, keepdims=True)
    slot2 = jnp.sum(jnp.where(pick2, slot, 0.0), axis=-1, keepdims=True)
    den = m1 + m2
    meta_ref[...] = jnp.where(lane_i == 0, slot1, jnp.where(lane_i == 1, slot2,
                              jnp.where(lane_i == 2, m1 / den, jnp.where(lane_i == 3, m2 / den, 0.0))))
    sub8 = lax.broadcasted_iota(jnp.int32, (SUBLANES, LANES), 0)
    info_ref[...] = jnp.where(sub8 == 0, start8, jnp.where(sub8 == 1, count8, 0.0))


def _route(l, j, x, mod, w, *, S, T):
    B, Tt, D = x.shape
    R = S * T
    assert R == MOE_BLOCK
    nt = Tt // T
    nblk = (B // S) * nt
    return pl.pallas_call(
        _route_kernel,
        grid=(B // S, nt),
        in_specs=[
            pl.BlockSpec((S, T, D), lambda b, t: (b, t, 0)),
            pl.BlockSpec((S, 6, D), lambda b, t: (b, 0, 0)),
            _resident((None, 1, D), lambda b, t: (l, 0, 0)),
            _resident((None, D, 2 * LANES), lambda b, t: (j, 0, 0)),
            _resident((None, 1, LANES), lambda b, t: (j, 0, 0)),
        ],
        out_specs=[
            pl.BlockSpec((S, T, D), lambda b, t: (b, t, 0)),
            pl.BlockSpec((None, R, LANES), lambda b, t: (b * nt + t, 0, 0)),
            pl.BlockSpec((None, SUBLANES, LANES), lambda b, t: (b * nt + t, 0, 0)),
        ],
        out_shape=[
            jax.ShapeDtypeStruct((B, Tt, D), F32),
            jax.ShapeDtypeStruct((nblk, R, LANES), F32),
            jax.ShapeDtypeStruct((nblk, SUBLANES, LANES), F32),
        ],
        compiler_params=pltpu.CompilerParams(
            dimension_semantics=("arbitrary", "arbitrary"),
            vmem_limit_bytes=VMEM_LIMIT_BYTES),
        name=f"route_S{S}",
    )(x, mod, w["norm2_g"], w["router_w"], w["router_b"])


def _moe_kernel(start_ref, count_ref, hn_ref, x_ref, g2_ref, slot_ref, gate_ref, wg_ref, wu_ref, wd_ref,
                *rest, final, tokens_per_seq):
    fg_ref = rest[0] if final else None
    o_ref, xs_tm, y_tm, ys, inv = rest[-5:]
    b = pl.program_id(0)
    e = pl.program_id(1)
    TB = hn_ref.shape[0]

    @pl.when(e == 0)
    def _index_slots():
        def clear(i, c):
            inv[i] = 0
            return c
        for ex in range(N_EXPERTS):
            lo = start_ref[b * N_EXPERTS + ex] + count_ref[b * N_EXPERTS + ex]
            lax.fori_loop(lo, lo + lax.rem(MOE_ROWS - lax.rem(count_ref[b * N_EXPERTS + ex], MOE_ROWS), MOE_ROWS), clear, 0)

        def fill(i, c):
            inv[slot_ref[0, TOP_K * i]] = i
            inv[slot_ref[0, TOP_K * i + 1]] = i
            return c
        lax.fori_loop(0, TB, fill, 0, unroll=SUBLANES)

    first = start_ref[b * N_EXPERTS + e]
    n_rows = count_ref[b * N_EXPERTS + e]
    n_chunks = lax.div(n_rows + (MOE_ROWS - 1), MOE_ROWS)

    def chunk(c, carry):
        base = first + c * MOE_ROWS

        def gather(r8, c2):
            for i in range(SUBLANES):
                _store_token(xs_tm, r8, i, hn_ref[inv[base + r8 * SUBLANES + i]])
            return c2
        lax.fori_loop(0, MOE_ROWS // SUBLANES, gather, 0)

        lhs = _tile_major_to_rows(xs_tm).astype(BF16)
        h = (_silu(_dot(lhs, wg_ref[...])) * _dot(lhs, wu_ref[...])).astype(BF16)
        _rows_to_tile_major(y_tm, _dot(h, wd_ref[...]))

        def park(r8, c2):
            for i in range(SUBLANES):
                ys[base + r8 * SUBLANES + i] = _load_token(y_tm, r8, i)
            return c2
        lax.fori_loop(0, MOE_ROWS // SUBLANES, park, 0)
        return carry

    lax.fori_loop(0, n_chunks, chunk, 0)

    @pl.when(e == N_EXPERTS - 1)
    def _combine():
        def comb(i8, c):
            for i in range(SUBLANES):
                t = i8 * SUBLANES + i
                f = (gate_ref[0, TOP_K * t] * ys[slot_ref[0, TOP_K * t]]
                     + gate_ref[0, TOP_K * t + 1] * ys[slot_ref[0, TOP_K * t + 1]])
                g2 = g2_ref[...] if tokens_per_seq >= TB else g2_ref[i8 // (tokens_per_seq // SUBLANES)]
                o_ref[t] = x_ref[t] + (1.0 + g2) * f
            return c
        lax.fori_loop(0, TB // SUBLANES, comb, 0)
        if final:
            y = o_ref[...]
            ms = jnp.sum(jnp.sum(y * y, axis=2, keepdims=True), axis=1, keepdims=True) * (1.0 / D_MODEL)
            o_ref[...] = y * lax.rsqrt(ms + EPS) * fg_ref[...]


def _moe(l, j, x, mod, w, final_g, *, S, T):
    B, Tt, D = x.shape
    ff = w["moe_w_gate"].shape[-1]
    final = final_g is not None
    N = B * Tt
    TB = MOE_BLOCK
    nblk = N // TB
    hn, meta, info = _route(l, j, x, mod, w, S=S, T=T)

    slots = meta[:, :, 0:TOP_K].astype(jnp.int32).reshape(nblk, 1, TOP_K * TB)
    gates = meta[:, :, TOP_K:2 * TOP_K].reshape(nblk, 1, TOP_K * TB)
    starts = info[:, 0, :N_EXPERTS].astype(jnp.int32).reshape(nblk * N_EXPERTS)
    counts = info[:, 1, :N_EXPERTS].astype(jnp.int32).reshape(nblk * N_EXPERTS)
    tok = lambda a: a.reshape(N, SUBLANES, LANES)
    mod_tok = mod.reshape(B, 6, SUBLANES, LANES)

    if Tt >= TB:
        blocks_per_seq = Tt // TB
        g2_spec = pl.BlockSpec((None, None, SUBLANES, LANES), lambda b, e, st, ct: (b // blocks_per_seq, 5, 0, 0))
    else:
        seqs = TB // Tt
        g2_spec = pl.BlockSpec((seqs, None, SUBLANES, LANES), lambda b, e, st, ct: (b, 5, 0, 0))
    smem_row = lambda: pl.BlockSpec((None, 1, TOP_K * TB), lambda b, e, st, ct: (b, 0, 0), memory_space=pltpu.SMEM)
    in_specs = [
        pl.BlockSpec((TB, SUBLANES, LANES), lambda b, e, st, ct: (b, 0, 0)),
        pl.BlockSpec((TB, SUBLANES, LANES), lambda b, e, st, ct: (b, 0, 0), pipeline_mode=pl.Buffered(1)),
        g2_spec,
        smem_row(), smem_row(),
        pl.BlockSpec((None, None, D, ff), lambda b, e, st, ct: (j, e, 0, 0)),
        pl.BlockSpec((None, None, D, ff), lambda b, e, st, ct: (j, e, 0, 0)),
        pl.BlockSpec((None, None, ff, D), lambda b, e, st, ct: (j, e, 0, 0)),
    ]
    args = [tok(hn), tok(x), mod_tok, slots, gates, w["moe_w_gate"], w["moe_w_up"], w["moe_w_down"]]
    if final:
        in_specs.append(pl.BlockSpec((SUBLANES, LANES), lambda b, e, st, ct: (0, 0)))
        args.append(final_g.reshape(SUBLANES, LANES))
    tile_major = pltpu.VMEM((MOE_ROWS // SUBLANES, LANE_BLOCKS * SUBLANES, LANES), F32)
    out = pl.pallas_call(
        functools.partial(_moe_kernel, final=final, tokens_per_seq=Tt),
        grid_spec=pltpu.PrefetchScalarGridSpec(
            num_scalar_prefetch=2,
            grid=(nblk, N_EXPERTS),
            in_specs=in_specs,
            out_specs=pl.BlockSpec((TB, SUBLANES, LANES), lambda b, e, st, ct: (b, 0, 0)),
            scratch_shapes=[
                tile_major, tile_major,
                pltpu.VMEM((MOE_SLOTS, SUBLANES, LANES), F32),
                pltpu.SMEM((MOE_SLOTS,), jnp.int32),
            ]),
        out_shape=jax.ShapeDtypeStruct((N, SUBLANES, LANES), F32),
        compiler_params=pltpu.CompilerParams(
            dimension_semantics=("arbitrary", "arbitrary"),
            vmem_limit_bytes=VMEM_LIMIT_BYTES),
        name=f"moe_S{S}",
    )(starts, counts, *args)
    return out.reshape(B, Tt, D)


def _block_diag_gate_weights(wa, wx):
    def bd(w):
        L, H, bs, _ = w.shape
        hg = GATE_GROUP // bs
        w5 = w.reshape(L, H // hg, hg, bs, bs)
        eye = jnp.eye(hg, dtype=w.dtype)
        m = jnp.einsum("lqhij,hk->lqhikj", w5, eye)
        return m.reshape(L, H // hg, GATE_GROUP, GATE_GROUP)
    return jnp.concatenate([bd(wa), bd(wx)], axis=-1)


def kernel(x_prompt, x_sample, c_prompt, c_sample, state_rglru_h, state_rglru_conv, state_cconv,
           ada_w, ada_b, norm1_g, norm2_g, w_in, rg_conv_w, rg_conv_b, rg_wa, rg_ba, rg_wx, rg_bx,
           rg_lambda, w_branch_a, cc_dw_w, cc_dw_b, cc_ln_g, cc_ln_b, w_branch_b, b_branch_b, w_out,
           ffn_w_gate, ffn_w_up, ffn_w_down, moe_router_w, moe_router_b, moe_w_gate, moe_w_up,
           moe_w_down, final_norm_g):
    depth = ada_w.shape[0]
    D = D_MODEL
    row = lambda a: a.reshape(a.shape[0], 1, a.shape[-1])
    n_moe = moe_router_w.shape[0]
    router_w = jnp.zeros((n_moe, D, LANES), F32).at[:, :, :N_EXPERTS].set(moe_router_w)
    router_hi = router_w.astype(BF16)
    router_lo = (router_w - router_hi.astype(F32)).astype(BF16)
    router_w = jnp.concatenate([router_hi, router_lo], axis=-1)
    router_b = jnp.full((n_moe, 1, LANES), -1e30, F32).at[:, 0, :N_EXPERTS].set(moe_router_b)
    w = {
        "norm1_g": row(norm1_g), "norm2_g": row(norm2_g),
        "w_in": w_in.astype(BF16),
        "rg_conv_w": rg_conv_w, "rg_conv_b": row(rg_conv_b),
        "gate_w": _block_diag_gate_weights(rg_wa, rg_wx).astype(BF16),
        "rg_ba": row(rg_ba), "rg_bx": row(rg_bx), "rg_lambda": row(rg_lambda),
        "w_branch_a": w_branch_a.astype(BF16),
        "cc_dw_w": jnp.tile(cc_dw_w.reshape(depth, CC_CONV_W, SUBLANES, LANES), (1, 1, 2, 1)).astype(BF16),
        "cc_dw_b": cc_dw_b.reshape(depth, SUBLANES, LANES),
        "cc_ln_g": row(cc_ln_g), "cc_ln_b": row(cc_ln_b),
        "w_branch_b": w_branch_b.astype(BF16), "b_branch_b": row(b_branch_b),
        "w_out": w_out.astype(BF16),
        "ffn_w_gate": ffn_w_gate.astype(BF16), "ffn_w_up": ffn_w_up.astype(BF16),
        "ffn_w_down": ffn_w_down.astype(BF16),
        "router_w": router_w, "router_b": router_b,
        "moe_w_gate": moe_w_gate.astype(BF16), "moe_w_up": moe_w_up.astype(BF16),
        "moe_w_down": moe_w_down.astype(BF16),
    }
    final_g = final_norm_g.reshape(1, D)

    nb_p = x_prompt.shape[0]
    nb_s = x_sample.shape[0]
    mod_all = _ada(jnp.concatenate([c_prompt, c_sample], axis=0), ada_w, ada_b)
    mod_all = mod_all.reshape(depth, nb_p + nb_s, 6, D)

    dt = x_prompt.dtype
    zeros_p = (jnp.zeros((depth, nb_p, SUBLANES, LANES), dt),
               jnp.zeros((depth, nb_p, RG_CONV_W - 1, D), dt),
               jnp.zeros((depth, nb_p, CC_CONV_W - 1, SUBLANES, LANES), dt))
    state_s = (state_rglru_h.reshape(depth, nb_s, SUBLANES, LANES), state_rglru_conv,
               state_cconv.reshape(depth, nb_s, CC_CONV_W - 1, SUBLANES, LANES))

    def trunk(x, mod, states, *, mix_tile, ffn_tile, moe_tile):
        new_h, new_rc, new_cc = [], [], []
        for l in range(depth):
            x, h, rc, cc = _mixer(l, x, mod[l], states[0][l], states[1][l], states[2][l], w,
                                  S=mix_tile[0], T=mix_tile[1])
            fg = final_g if l == depth - 1 else None
            if l % 2 == 0:
                x = _ffn(l, l // 2, x, mod[l], w, fg, S=ffn_tile[0], T=ffn_tile[1])
            else:
                x = _moe(l, l // 2, x, mod[l], w, fg, S=moe_tile[0], T=moe_tile[1])
            new_h.append(h.reshape(h.shape[0], D))
            new_rc.append(rc)
            new_cc.append(cc.reshape(cc.shape[0], CC_CONV_W - 1, D))
        return x, jnp.stack(new_h), jnp.stack(new_rc), jnp.stack(new_cc)

    y_p, h_p, rc_p, cc_p = trunk(x_prompt, mod_all[:, :nb_p], zeros_p,
                                 mix_tile=(1, 256), ffn_tile=(1, 512), moe_tile=(1, MOE_BLOCK))
    y_s, h_s, rc_s, cc_s = trunk(x_sample, mod_all[:, nb_p:], state_s,
                                 mix_tile=(16, 8), ffn_tile=(64, 8), moe_tile=(MOE_BLOCK // 8, 8))
    return (y_p, y_s, h_p, rc_p, cc_p, h_s, rc_s, cc_s)
```

```python
import functools

import jax
import jax.numpy as jnp
from jax import lax
from jax.experimental import pallas as pl
from jax.experimental.pallas import tpu as pltpu

D_MODEL = 1024
N_RNN_HEADS = 16
RG_CONV_W = 4
RG_C = 8.0
CC_CONV_W = 31
N_EXPERTS = 8
TOP_K = 2
EPS = 1e-6

LANES = 128
SUBLANES = 8
VMEM_LIMIT_BYTES = 56 * 1024 * 1024

GATE_GROUP = 256
RC_HIST = SUBLANES
CC_HIST = 32
CONV_BLOCK = 4

BF16 = jnp.bfloat16
F32 = jnp.float32


def _sigmoid(x):
    return jax.nn.sigmoid(x)


def _silu(x):
    return x * _sigmoid(x)


def _gelu_tanh(x):
    c = 0.7978845608028654
    return 0.5 * x * (1.0 + jnp.tanh(c * (x + 0.044715 * (x * x * x))))


def _softplus(x):
    return jnp.maximum(x, 0.0) + jnp.log1p(jnp.exp(-jnp.abs(x)))


def _rmsnorm(x, g):
    ms = jnp.mean(x * x, axis=-1, keepdims=True)
    return x * lax.rsqrt(ms + EPS) * g


def _dot(a, b):
    return jnp.dot(a, b, preferred_element_type=F32)


def _resident(block_shape, index_map):
    return pl.BlockSpec(block_shape, index_map, pipeline_mode=pl.Buffered(1))


def _ada_kernel(c_ref, w_ref, b_ref, o_ref):
    s = _silu(c_ref[...]).astype(BF16)
    o_ref[...] = _dot(s, w_ref[...].astype(BF16)) + b_ref[...]


def _ada(c_all, ada_w, ada_b):
    depth, d, n = ada_w.shape
    nb = c_all.shape[0]
    tn = 1536
    return pl.pallas_call(
        _ada_kernel,
        grid=(depth, n // tn),
        in_specs=[
            pl.BlockSpec((nb, d), lambda l, j: (0, 0)),
            pl.BlockSpec((None, d, tn), lambda l, j: (l, 0, j)),
            pl.BlockSpec((None, 1, tn), lambda l, j: (l, 0, j)),
        ],
        out_specs=pl.BlockSpec((None, nb, tn), lambda l, j: (l, 0, j)),
        out_shape=jax.ShapeDtypeStruct((depth, nb, n), F32),
        compiler_params=pltpu.CompilerParams(
            dimension_semantics=("arbitrary", "arbitrary"),
            vmem_limit_bytes=VMEM_LIMIT_BYTES),
        name="ada_mod",
    )(c_all, ada_w, ada_b.reshape(depth, 1, n))


LANE_BLOCKS = D_MODEL // LANES


def _rows_to_tile_major(tm_ref, val, lane_block0=0, row_block0=0):
    nb = val.shape[0] // SUBLANES
    for j in range(val.shape[1] // LANES):
        lo = (lane_block0 + j) * SUBLANES
        tm_ref[row_block0:row_block0 + nb, lo:lo + SUBLANES, :] = (
            val[:, j * LANES:(j + 1) * LANES].reshape(nb, SUBLANES, LANES))


def _tile_major_to_rows(tm_ref, row_block0=0, n_blocks=None):
    nb = tm_ref.shape[0] if n_blocks is None else n_blocks
    return jnp.concatenate(
        [tm_ref[row_block0:row_block0 + nb, j * SUBLANES:(j + 1) * SUBLANES, :].reshape(nb * SUBLANES, LANES)
         for j in range(LANE_BLOCKS)], axis=1)


def _load_token(tm_ref, blk, sub):
    return tm_ref[blk, pl.ds(sub, SUBLANES, stride=SUBLANES), :]


def _store_token(tm_ref, blk, sub, val):
    tm_ref[blk, pl.ds(sub, SUBLANES, stride=SUBLANES), :] = val


def _mixer_kernel(x_ref, mod_ref, h0_ref, rc0_ref, cc0_ref,
                  n1g_ref, win_ref, rgw_ref, rgb_ref, gw_ref, ba_ref, bx_ref,
                  lam_ref, wa_ref, ccw_ref, ccb_ref, lng_ref, lnb_ref, wb_ref,
                  bb_ref, wo_ref,
                  xo_ref, ho_ref, rco_ref, cco_ref,
                  hn_s, xbuf, xc_s, a_tm, b_tm, u_tm, y_tm, vt, vtb, hcar, st_tm,
                  *, S, T, n_chunks):
    C = D_MODEL
    R = S * T
    HT = CC_HIST + T
    t_idx = pl.program_id(1)

    @pl.when(t_idx == 0)
    def _load_state():
        xbuf[:, RC_HIST - (RG_CONV_W - 1):RC_HIST, :] = rc0_ref[...]
        pad_rows = jnp.zeros((CC_HIST - (CC_CONV_W - 1), C), F32)
        for s in range(S):
            _rows_to_tile_major(st_tm, jnp.concatenate([cc0_ref[s], pad_rows], axis=0))
            for p in range(CC_CONV_W - 1):
                vt[s * HT + CC_HIST - (CC_CONV_W - 1) + p] = _load_token(st_tm, *divmod(p, SUBLANES))
            _rows_to_tile_major(st_tm, jnp.broadcast_to(h0_ref[s], (SUBLANES, C)))
            hcar[s] = _load_token(st_tm, 0, 0)

    x3 = x_ref[...]
    sh1 = mod_ref[:, 0:1, :]
    sc1 = mod_ref[:, 1:2, :]
    g1 = mod_ref[:, 2:3, :]
    hn = _rmsnorm(x3, n1g_ref[...]) * (1.0 + sc1) + sh1
    hn_s[...] = hn.reshape(R, C).astype(BF16)

    def proj(j):
        return _dot(hn_s[...], win_ref[:, j * C:(j + 1) * C])

    xbuf[:, RC_HIST:RC_HIST + T, :] = proj(0).reshape(S, T, C)
    xc3 = rgb_ref[...] + rgw_ref[0:1, :] * xbuf[:, RC_HIST - 3:RC_HIST - 3 + T, :]
    for k in range(1, RG_CONV_W):
        lo = RC_HIST - 3 + k
        xc3 = xc3 + rgw_ref[k:k + 1, :] * xbuf[:, lo:lo + T, :]
    xc_s[...] = xc3.reshape(R, C)

    for g in range(C // GATE_GROUP):
        cs = slice(g * GATE_GROUP, (g + 1) * GATE_GROUP)
        xg = xc_s[:, cs]
        pre = _dot(xg.astype(BF16), gw_ref[g])
        r = _sigmoid(pre[:, :GATE_GROUP] + ba_ref[:, cs])
        i = _sigmoid(pre[:, GATE_GROUP:] + bx_ref[:, cs])
        log_a = (-RG_C * _softplus(-lam_ref[:, cs])) * r
        a = jnp.exp(log_a)
        mult = jnp.sqrt(jnp.tanh(-log_a) * (a * a + 1.0))
        _rows_to_tile_major(a_tm, a, g * (GATE_GROUP // LANES))
        _rows_to_tile_major(b_tm, mult * (i * xg), g * (GATE_GROUP // LANES))

    for s in range(S):
        h = hcar[s]
        for t in range(T):
            blk, sub = divmod(s * T + t, SUBLANES)
            h = _load_token(a_tm, blk, sub) * h + _load_token(b_tm, blk, sub)
            _store_token(b_tm, blk, sub, h)
        hcar[s] = h

    ya_in = (_gelu_tanh(proj(1)) * _tile_major_to_rows(b_tm)).astype(BF16)
    ya = _dot(ya_in, wa_ref[...])

    _rows_to_tile_major(u_tm, proj(2) * _sigmoid(proj(3)))
    for s in range(S):
        for t in range(T):
            blk, sub = divmod(s * T + t, SUBLANES)
            vt[s * HT + CC_HIST + t] = _load_token(u_tm, blk, sub)

    n_streams, Lp = (1, T // 2) if S == 1 else (S // 2, T)
    hi_off = T // 2 if S == 1 else (S // 2) * HT
    HP = CC_HIST + Lp
    first_tap = CC_HIST - (CC_CONV_W - 1)
    for q in range(n_streams):
        for p in range(first_tap, HP):
            lo = q * HT + p
            vtb[q * HP + p] = jnp.concatenate([vt[lo], vt[lo + hi_off]], axis=0).astype(BF16)

    taps = [ccw_ref[k] for k in range(CC_CONV_W)]
    bias = ccb_ref[...]
    blocks_per_stream = Lp // SUBLANES

    def conv_rows(bi, carry):
        q = 0 if n_streams == 1 else bi // blocks_per_stream
        base = bi * SUBLANES + q * CC_HIST + first_tap
        for i0 in range(0, SUBLANES, CONV_BLOCK):
            acc = [None] * CONV_BLOCK
            for j in range(CONV_BLOCK + CC_CONV_W - 1):
                xj = vtb[base + i0 + j].astype(F32)
                for i in range(CONV_BLOCK):
                    k = j - i
                    if 0 <= k < CC_CONV_W:
                        prod = xj * taps[k].astype(F32)
                        acc[i] = prod if acc[i] is None else acc[i] + prod
            for i in range(CONV_BLOCK):
                _store_token(y_tm, bi, i0 + i, acc[i][:SUBLANES] + bias)
                _store_token(y_tm, bi + R // (2 * SUBLANES), i0 + i, acc[i][SUBLANES:] + bias)
        return carry

    lax.fori_loop(0, R // (2 * SUBLANES), conv_rows, 0)

    v = _tile_major_to_rows(y_tm)
    mu = jnp.mean(v, axis=-1, keepdims=True)
    vc = v - mu
    vn = vc * lax.rsqrt(jnp.mean(vc * vc, axis=-1, keepdims=True) + EPS)
    vn = _silu(vn * lng_ref[...] + lnb_ref[...])
    yb = _dot(vn.astype(BF16), wb_ref[...]) + bb_ref[...]

    m = _sigmoid(proj(4)) * ya + _sigmoid(proj(5)) * yb
    o = _dot(m.astype(BF16), wo_ref[...])
    xo_ref[...] = x3 + (1.0 + g1) * o.reshape(S, T, C)

    @pl.when(t_idx == n_chunks - 1)
    def _write_state():
        rco_ref[...] = xbuf[:, RC_HIST + T - (RG_CONV_W - 1):RC_HIST + T, :]
        for s in range(S):
            _store_token(st_tm, 0, 0, hcar[s])
            ho_ref[s] = _tile_major_to_rows(st_tm, 0, 1)[0:1, :]
            for p in range(CC_CONV_W - 1):
                _store_token(st_tm, *divmod(p, SUBLANES), vt[s * HT + HT - (CC_CONV_W - 1) + p])
            cco_ref[s] = _tile_major_to_rows(st_tm)[0:CC_CONV_W - 1, :]

    if n_chunks > 1:
        xbuf[:, 0:RC_HIST, :] = xbuf[:, T:T + RC_HIST, :]
        for s in range(S):
            vt[s * HT:s * HT + CC_HIST] = vt[s * HT + T:s * HT + T + CC_HIST]


def _mixer(l, x, mod, h0, rc0, cc0, w, *, S, T):
    B, Tt, D = x.shape
    C = D
    n_chunks = Tt // T
    assert B % S == 0 and Tt % T == 0 and T % SUBLANES == 0
    assert S == 1 or n_chunks == 1
    assert n_chunks == 1 or T >= CC_HIST
    R = S * T

    def seq(*tail):
        return pl.BlockSpec((S,) + tail, lambda b, t: (b,) + (0,) * len(tail))

    def layer(*tail):
        return _resident((None,) + tail, lambda b, t: (l,) + (0,) * len(tail))

    in_specs = [
        pl.BlockSpec((S, T, D), lambda b, t: (b, t, 0)),
        seq(6, D),
        seq(1, C), seq(RG_CONV_W - 1, C), seq(CC_CONV_W - 1, C),
        layer(1, D),
        layer(D, 6 * C),
        layer(RG_CONV_W, C), layer(1, C),
        layer(C // GATE_GROUP, GATE_GROUP, 2 * GATE_GROUP),
        layer(1, C), layer(1, C), layer(1, C),
        layer(C, D),
        layer(CC_CONV_W, 2 * SUBLANES, LANES), layer(SUBLANES, LANES),
        layer(1, C), layer(1, C),
        layer(C, D), layer(1, D),
        layer(D, D),
    ]
    out_specs = [
        pl.BlockSpec((S, T, D), lambda b, t: (b, t, 0)),
        seq(1, C), seq(RG_CONV_W - 1, C), seq(CC_CONV_W - 1, C),
    ]
    out_shape = [
        jax.ShapeDtypeStruct((B, Tt, D), F32),
        jax.ShapeDtypeStruct((B, 1, C), F32),
        jax.ShapeDtypeStruct((B, RG_CONV_W - 1, C), F32),
        jax.ShapeDtypeStruct((B, CC_CONV_W - 1, C), F32),
    ]
    tile_major = pltpu.VMEM((R // SUBLANES, LANE_BLOCKS * SUBLANES, LANES), F32)
    scratch = [
        pltpu.VMEM((R, D), BF16),
        pltpu.VMEM((S, RC_HIST + T, C), F32),
        pltpu.VMEM((R, C), F32),
        tile_major, tile_major, tile_major, tile_major,
        pltpu.VMEM((S * (CC_HIST + T), SUBLANES, LANES), F32),
        pltpu.VMEM((R // 2 + CC_HIST * (1 if S == 1 else S // 2), 2 * SUBLANES, LANES), BF16),
        pltpu.VMEM((S, SUBLANES, LANES), F32),
        pltpu.VMEM((CC_HIST // SUBLANES, LANE_BLOCKS * SUBLANES, LANES), F32),
    ]
    return pl.pallas_call(
        functools.partial(_mixer_kernel, S=S, T=T, n_chunks=n_chunks),
        grid=(B // S, n_chunks),
        in_specs=in_specs,
        out_specs=out_specs,
        out_shape=out_shape,
        scratch_shapes=scratch,
        compiler_params=pltpu.CompilerParams(
            dimension_semantics=("arbitrary", "arbitrary"),
            vmem_limit_bytes=VMEM_LIMIT_BYTES),
        name=f"mixer_S{S}",
    )(x, mod, h0, rc0, cc0,
      w["norm1_g"], w["w_in"], w["rg_conv_w"], w["rg_conv_b"], w["gate_w"],
      w["rg_ba"], w["rg_bx"], w["rg_lambda"], w["w_branch_a"],
      w["cc_dw_w"], w["cc_dw_b"], w["cc_ln_g"], w["cc_ln_b"], w["w_branch_b"],
      w["b_branch_b"], w["w_out"])


def _finish(x3, g2, f, fg_ref, o_ref):
    y = x3 + (1.0 + g2) * f.reshape(x3.shape)
    if fg_ref is not None:
        y = _rmsnorm(y, fg_ref[...])
    o_ref[...] = y


def _ffn_kernel(x_ref, mod_ref, n2g_ref, wg_ref, wu_ref, wd_ref, *rest, n_split, final):
    fg_ref = rest[0] if final else None
    o_ref = rest[-1]
    x3 = x_ref[...]
    S, T, D = x3.shape
    sh2 = mod_ref[:, 3:4, :]
    sc2 = mod_ref[:, 4:5, :]
    g2 = mod_ref[:, 5:6, :]
    hn = (_rmsnorm(x3, n2g_ref[...]) * (1.0 + sc2) + sh2).reshape(S * T, D).astype(BF16)
    ff = wg_ref.shape[1] // n_split
    f = None
    for j in range(n_split):
        cs = slice(j * ff, (j + 1) * ff)
        h = (_silu(_dot(hn, wg_ref[:, cs])) * _dot(hn, wu_ref[:, cs])).astype(BF16)
        part = _dot(h, wd_ref[cs, :])
        f = part if f is None else f + part
    _finish(x3, g2, f, fg_ref, o_ref)


def _ffn(l, j, x, mod, w, final_g, *, S, T):
    B, Tt, D = x.shape
    ff = w["ffn_w_gate"].shape[-1]
    final = final_g is not None
    in_specs = [
        pl.BlockSpec((S, T, D), lambda b, t: (b, t, 0)),
        pl.BlockSpec((S, 6, D), lambda b, t: (b, 0, 0)),
        _resident((None, 1, D), lambda b, t: (l, 0, 0)),
        _resident((None, D, ff), lambda b, t: (j, 0, 0)),
        _resident((None, D, ff), lambda b, t: (j, 0, 0)),
        _resident((None, ff, D), lambda b, t: (j, 0, 0)),
    ]
    args = [x, mod, w["norm2_g"], w["ffn_w_gate"], w["ffn_w_up"], w["ffn_w_down"]]
    if final:
        in_specs.append(_resident((1, D), lambda b, t: (0, 0)))
        args.append(final_g)
    return pl.pallas_call(
        functools.partial(_ffn_kernel, n_split=2, final=final),
        grid=(B // S, Tt // T),
        in_specs=in_specs,
        out_specs=pl.BlockSpec((S, T, D), lambda b, t: (b, t, 0)),
        out_shape=jax.ShapeDtypeStruct((B, Tt, D), F32),
        compiler_params=pltpu.CompilerParams(
            dimension_semantics=("arbitrary", "arbitrary"),
            vmem_limit_bytes=VMEM_LIMIT_BYTES),
        name=f"ffn_S{S}",
    )(*args)


MOE_BLOCK = 1024
MOE_ROWS = 128
MOE_SLOTS = TOP_K * MOE_BLOCK + N_EXPERTS * MOE_ROWS


MOE_FINISH_ROWS = 256


def _adaln2(x3, mod_ref, n2g_ref, rows):
    return _rmsnorm(x3, n2g_ref[...]) * (1.0 + mod_ref[rows, 4:5, :]) + mod_ref[rows, 3:4, :]


def _route_kernel(x_ref, mod_ref, n2g_ref, wr_ref, br_ref, meta_ref, info_ref):
    x3 = x_ref[...]
    S, T, D = x3.shape
    R = S * T
    hn = _adaln2(x3, mod_ref, n2g_ref, slice(None)).reshape(R, D)
    hn_hi = hn.astype(BF16)
    hn_lo = (hn - hn_hi.astype(F32)).astype(BF16)
    both = _dot(hn_hi, wr_ref[...])
    logits = both[:, :LANES] + both[:, LANES:] + _dot(hn_lo, wr_ref[:, :LANES]) + br_ref[...]
    z = logits - jnp.max(logits, axis=-1, keepdims=True)
    p = jnp.exp(z)
    p = p / jnp.sum(p, axis=-1, keepdims=True)
    lane_i = lax.broadcasted_iota(jnp.int32, p.shape, 1)
    lane = lane_i.astype(F32)
    far = jnp.float32(p.shape[1])
    m1 = jnp.max(p, axis=-1, keepdims=True)
    i1 = jnp.min(jnp.where(p == m1, lane, far), axis=-1, keepdims=True)
    pick1 = lane == i1
    p2 = jnp.where(pick1, -1.0, p)
    m2 = jnp.max(p2, axis=-1, keepdims=True)
    i2 = jnp.min(jnp.where(p2 == m2, lane, far), axis=-1, keepdims=True)
    pick2 = lane == i2
    onehot = jnp.where(pick1, 1.0, 0.0) + jnp.where(pick2, 1.0, 0.0)
    earlier = (lax.broadcasted_iota(jnp.int32, (R, R), 1) < lax.broadcasted_iota(jnp.int32, (R, R), 0))
    rank = _dot(jnp.where(earlier, 1.0, 0.0).astype(BF16), onehot.astype(BF16))
    count = rank[R - 1:R, :] + onehot[R - 1:R, :]
    count8 = jnp.broadcast_to(count, (SUBLANES, LANES))
    padded = jnp.floor((count8 + (MOE_ROWS - 1)) * (1.0 / MOE_ROWS)) * MOE_ROWS
    incl = padded
    lane8 = lax.broadcasted_iota(jnp.int32, (SUBLANES, LANES), 1)
    for k in (1, 2, 4):
        incl = incl + jnp.where(lane8 >= k, pltpu.roll(incl, k, 1), 0.0)
    start8 = incl - padded
    slot = start8[0:1, :] + rank
    slot1 = jnp.sum(jnp.where(pick1, slot, 0.0), axis=-1, keepdims=True)
    slot2 = jnp.sum(jnp.where(pick2, slot, 0.0), axis=-1, keepdims=True)
    den = m1 + m2
    meta_ref[...] = jnp.where(lane_i == 0, slot1, jnp.where(lane_i == 1, slot2,
                              jnp.where(lane_i == 2, m1 / den, jnp.where(lane_i == 3, m2 / den, 0.0))))
    sub8 = lax.broadcasted_iota(jnp.int32, (SUBLANES, LANES), 0)
    info_ref[...] = jnp.where(sub8 == 0, start8, jnp.where(sub8 == 1, count8, 0.0))


def _route(l, j, x, mod, w, *, S, T):
    B, Tt, D = x.shape
    R = S * T
    assert R == MOE_BLOCK
    nt = Tt // T
    nblk = (B // S) * nt
    return pl.pallas_call(
        _route_kernel,
        grid=(B // S, nt),
        in_specs=[
            pl.BlockSpec((S, T, D), lambda b, t: (b, t, 0)),
            pl.BlockSpec((S, 6, D), lambda b, t: (b, 0, 0)),
            _resident((None, 1, D), lambda b, t: (l, 0, 0)),
            _resident((None, D, 2 * LANES), lambda b, t: (j, 0, 0)),
            _resident((None, 1, LANES), lambda b, t: (j, 0, 0)),
        ],
        out_specs=[
            pl.BlockSpec((None, R, LANES), lambda b, t: (b * nt + t, 0, 0)),
            pl.BlockSpec((None, SUBLANES, LANES), lambda b, t: (b * nt + t, 0, 0)),
        ],
        out_shape=[
            jax.ShapeDtypeStruct((nblk, R, LANES), F32),
            jax.ShapeDtypeStruct((nblk, SUBLANES, LANES), F32),
        ],
        compiler_params=pltpu.CompilerParams(
            dimension_semantics=("arbitrary", "arbitrary"),
            vmem_limit_bytes=VMEM_LIMIT_BYTES),
        name=f"route_S{S}",
    )(x, mod, w["norm2_g"], w["router_w"], w["router_b"])


def _row_chunks(S, T, rows):
    if S == 1:
        return [(slice(0, 1), slice(r0, r0 + rows), r0) for r0 in range(0, T, rows)]
    per = rows // T
    return [(slice(s0, s0 + per), slice(None), s0 * T) for s0 in range(0, S, per)]


def _moe_kernel(start_ref, count_ref, x_ref, mod_ref, n2g_ref, slot_ref, gate_ref, wg_ref, wu_ref, wd_ref,
                *rest, final):
    fg_ref = rest[0] if final else None
    o_ref, hn_tm, xs_tm, y_tm, ys, inv = rest[-6:]
    b = pl.program_id(0)
    e = pl.program_id(1)
    S, T, D = x_ref.shape
    TB = S * T
    pieces = _row_chunks(S, T, MOE_FINISH_ROWS)

    @pl.when(e == 0)
    def _prepare():
        for ss, ts, r0 in pieces:
            hn = _adaln2(x_ref[ss, ts, :], mod_ref, n2g_ref, ss)
            _rows_to_tile_major(hn_tm, hn.reshape(MOE_FINISH_ROWS, D), 0, r0 // SUBLANES)

        def clear(i, c):
            inv[i] = 0
            return c
        for ex in range(N_EXPERTS):
            cnt = count_ref[b * N_EXPERTS + ex]
            lo = start_ref[b * N_EXPERTS + ex] + cnt
            lax.fori_loop(lo, lo + lax.rem(MOE_ROWS - lax.rem(cnt, MOE_ROWS), MOE_ROWS), clear, 0)

        def fill(i, c):
            inv[slot_ref[0, TOP_K * i]] = i
            inv[slot_ref[0, TOP_K * i + 1]] = i
            return c
        lax.fori_loop(0, TB, fill, 0, unroll=SUBLANES)

    first = start_ref[b * N_EXPERTS + e]
    n_rows = count_ref[b * N_EXPERTS + e]
    n_chunks = lax.div(n_rows + (MOE_ROWS - 1), MOE_ROWS)

    def chunk(c, carry):
        base = first + c * MOE_ROWS

        def gather(r8, c2):
            for i in range(SUBLANES):
                t = inv[base + r8 * SUBLANES + i]
                _store_token(xs_tm, r8, i, _load_token(hn_tm, lax.shift_right_logical(t, 3), lax.bitwise_and(t, 7)))
            return c2
        lax.fori_loop(0, MOE_ROWS // SUBLANES, gather, 0)

        lhs = _tile_major_to_rows(xs_tm).astype(BF16)
        h = (_silu(_dot(lhs, wg_ref[...])) * _dot(lhs, wu_ref[...])).astype(BF16)
        _rows_to_tile_major(y_tm, _dot(h, wd_ref[...]))

        def park(r8, c2):
            for i in range(SUBLANES):
                ys[base + r8 * SUBLANES + i] = _load_token(y_tm, r8, i)
            return c2
        lax.fori_loop(0, MOE_ROWS // SUBLANES, park, 0)
        return carry

    lax.fori_loop(0, n_chunks, chunk, 0)

    @pl.when(e == N_EXPERTS - 1)
    def _combine():
        def comb(i8, c):
            for i in range(SUBLANES):
                t = i8 * SUBLANES + i
                f = (gate_ref[0, TOP_K * t] * ys[slot_ref[0, TOP_K * t]]
                     + gate_ref[0, TOP_K * t + 1] * ys[slot_ref[0, TOP_K * t + 1]])
                _store_token(hn_tm, i8, i, f)
            return c
        lax.fori_loop(0, TB // SUBLANES, comb, 0)
        for ss, ts, r0 in pieces:
            x3 = x_ref[ss, ts, :]
            f = _tile_major_to_rows(hn_tm, r0 // SUBLANES, MOE_FINISH_ROWS // SUBLANES)
            y = x3 + (1.0 + mod_ref[ss, 5:6, :]) * f.reshape(x3.shape)
            if final:
                y = _rmsnorm(y, fg_ref[...])
            o_ref[ss, ts, :] = y


def _moe(l, j, x, mod, w, final_g, *, S, T):
    B, Tt, D = x.shape
    ff = w["moe_w_gate"].shape[-1]
    final = final_g is not None
    TB = S * T
    nt = Tt // T
    nblk = (B // S) * nt
    meta, info = _route(l, j, x, mod, w, S=S, T=T)

    slots = meta[:, :, 0:TOP_K].astype(jnp.int32).reshape(nblk, 1, TOP_K * TB)
    gates = meta[:, :, TOP_K:2 * TOP_K].reshape(nblk, 1, TOP_K * TB)
    starts = info[:, 0, :N_EXPERTS].astype(jnp.int32).reshape(nblk * N_EXPERTS)
    counts = info[:, 1, :N_EXPERTS].astype(jnp.int32).reshape(nblk * N_EXPERTS)

    smem_row = lambda: pl.BlockSpec((None, 1, TOP_K * TB), lambda b, e, st, ct: (b, 0, 0), memory_space=pltpu.SMEM)
    in_specs = [
        pl.BlockSpec((S, T, D), lambda b, e, st, ct: (b // nt, b % nt, 0), pipeline_mode=pl.Buffered(1)),
        pl.BlockSpec((S, 6, D), lambda b, e, st, ct: (b // nt, 0, 0)),
        pl.BlockSpec((None, 1, D), lambda b, e, st, ct: (l, 0, 0)),
        smem_row(), smem_row(),
        pl.BlockSpec((None, None, D, ff), lambda b, e, st, ct: (j, e, 0, 0)),
        pl.BlockSpec((None, None, D, ff), lambda b, e, st, ct: (j, e, 0, 0)),
        pl.BlockSpec((None, None, ff, D), lambda b, e, st, ct: (j, e, 0, 0)),
    ]
    args = [x, mod, w["norm2_g"], slots, gates, w["moe_w_gate"], w["moe_w_up"], w["moe_w_down"]]
    if final:
        in_specs.append(pl.BlockSpec((1, D), lambda b, e, st, ct: (0, 0)))
        args.append(final_g)
    tile_major = lambda rows: pltpu.VMEM((rows // SUBLANES, LANE_BLOCKS * SUBLANES, LANES), F32)
    return pl.pallas_call(
        functools.partial(_moe_kernel, final=final),
        grid_spec=pltpu.PrefetchScalarGridSpec(
            num_scalar_prefetch=2,
            grid=(nblk, N_EXPERTS),
            in_specs=in_specs,
            out_specs=pl.BlockSpec((S, T, D), lambda b, e, st, ct: (b // nt, b % nt, 0)),
            scratch_shapes=[
                tile_major(TB),
                tile_major(MOE_ROWS), tile_major(MOE_ROWS),
                pltpu.VMEM((MOE_SLOTS, SUBLANES, LANES), F32),
                pltpu.SMEM((MOE_SLOTS,), jnp.int32),
            ]),
        out_shape=jax.ShapeDtypeStruct((B, Tt, D), F32),
        compiler_params=pltpu.CompilerParams(
            dimension_semantics=("arbitrary", "arbitrary"),
            vmem_limit_bytes=VMEM_LIMIT_BYTES),
        name=f"moe_S{S}",
    )(starts, counts, *args)


def _block_diag_gate_weights(wa, wx):
    def bd(w):
        L, H, bs, _ = w.shape
        hg = GATE_GROUP // bs
        w5 = w.reshape(L, H // hg, hg, bs, bs)
        eye = jnp.eye(hg, dtype=w.dtype)
        m = jnp.einsum("lqhij,hk->lqhikj", w5, eye)
        return m.reshape(L, H // hg, GATE_GROUP, GATE_GROUP)
    return jnp.concatenate([bd(wa), bd(wx)], axis=-1)


def kernel(x_prompt, x_sample, c_prompt, c_sample, state_rglru_h, state_rglru_conv, state_cconv,
           ada_w, ada_b, norm1_g, norm2_g, w_in, rg_conv_w, rg_conv_b, rg_wa, rg_ba, rg_wx, rg_bx,
           rg_lambda, w_branch_a, cc_dw_w, cc_dw_b, cc_ln_g, cc_ln_b, w_branch_b, b_branch_b, w_out,
           ffn_w_gate, ffn_w_up, ffn_w_down, moe_router_w, moe_router_b, moe_w_gate, moe_w_up,
           moe_w_down, final_norm_g):
    depth = ada_w.shape[0]
    D = D_MODEL
    row = lambda a: a.reshape(a.shape[0], 1, a.shape[-1])
    n_moe = moe_router_w.shape[0]
    router_w = jnp.zeros((n_moe, D, LANES), F32).at[:, :, :N_EXPERTS].set(moe_router_w)
    router_hi = router_w.astype(BF16)
    router_lo = (router_w - router_hi.astype(F32)).astype(BF16)
    router_w = jnp.concatenate([router_hi, router_lo], axis=-1)
    router_b = jnp.full((n_moe, 1, LANES), -1e30, F32).at[:, 0, :N_EXPERTS].set(moe_router_b)
    w = {
        "norm1_g": row(norm1_g), "norm2_g": row(norm2_g),
        "w_in": w_in.astype(BF16),
        "rg_conv_w": rg_conv_w, "rg_conv_b": row(rg_conv_b),
        "gate_w": _block_diag_gate_weights(rg_wa, rg_wx).astype(BF16),
        "rg_ba": row(rg_ba), "rg_bx": row(rg_bx), "rg_lambda": row(rg_lambda),
        "w_branch_a": w_branch_a.astype(BF16),
        "cc_dw_w": jnp.tile(cc_dw_w.reshape(depth, CC_CONV_W, SUBLANES, LANES), (1, 1, 2, 1)).astype(BF16),
        "cc_dw_b": cc_dw_b.reshape(depth, SUBLANES, LANES),
        "cc_ln_g": row(cc_ln_g), "cc_ln_b": row(cc_ln_b),
        "w_branch_b": w_branch_b.astype(BF16), "b_branch_b": row(b_branch_b),
        "w_out": w_out.astype(BF16),
        "ffn_w_gate": ffn_w_gate.astype(BF16), "ffn_w_up": ffn_w_up.astype(BF16),
        "ffn_w_down": ffn_w_down.astype(BF16),
        "router_w": router_w, "router_b": router_b,
        "moe_w_gate": moe_w_gate.astype(BF16), "moe_w_up": moe_w_up.astype(BF16),
        "moe_w_down": moe_w_down.astype(BF16),
    }
    final_g = final_norm_g.reshape(1, D)

    nb_p = x_prompt.shape[0]
    nb_s = x_sample.shape[0]
    mod_all = _ada(jnp.concatenate([c_prompt, c_sample], axis=0), ada_w, ada_b)
    mod_all = mod_all.reshape(depth, nb_p + nb_s, 6, D)

    dt = x_prompt.dtype
    zeros_p = (jnp.zeros((depth, nb_p, 1, D), dt),
               jnp.zeros((depth, nb_p, RG_CONV_W - 1, D), dt),
               jnp.zeros((depth, nb_p, CC_CONV_W - 1, D), dt))
    state_s = (state_rglru_h.reshape(depth, nb_s, 1, D), state_rglru_conv, state_cconv)

    def trunk(x, mod, states, *, mix_tile, ffn_tile, moe_tile):
        new_h, new_rc, new_cc = [], [], []
        for l in range(depth):
            x, h, rc, cc = _mixer(l, x, mod[l], states[0][l], states[1][l], states[2][l], w,
                                  S=mix_tile[0], T=mix_tile[1])
            fg = final_g if l == depth - 1 else None
            if l % 2 == 0:
                x = _ffn(l, l // 2, x, mod[l], w, fg, S=ffn_tile[0], T=ffn_tile[1])
            else:
                x = _moe(l, l // 2, x, mod[l], w, fg, S=moe_tile[0], T=moe_tile[1])
            new_h.append(h[:, 0, :])
            new_rc.append(rc)
            new_cc.append(cc)
        return x, jnp.stack(new_h), jnp.stack(new_rc), jnp.stack(new_cc)

    y_p, h_p, rc_p, cc_p = trunk(x_prompt, mod_all[:, :nb_p], zeros_p,
                                 mix_tile=(1, 256), ffn_tile=(1, 512), moe_tile=(1, MOE_BLOCK))
    y_s, h_s, rc_s, cc_s = trunk(x_sample, mod_all[:, nb_p:], state_s,
                                 mix_tile=(16, 8), ffn_tile=(64, 8), moe_tile=(MOE_BLOCK // 8, 8))
    return (y_p, y_s, h_p, rc_p, cc_p, h_s, rc_s, cc_s)
```

```python
import functools

import jax
import jax.numpy as jnp
from jax import lax
from jax.experimental import pallas as pl
from jax.experimental.pallas import tpu as pltpu

D_MODEL = 1024
N_RNN_HEADS = 16
RG_CONV_W = 4
RG_C = 8.0
CC_CONV_W = 31
N_EXPERTS = 8
TOP_K = 2
EPS = 1e-6

LANES = 128
SUBLANES = 8
VMEM_LIMIT_BYTES = 56 * 1024 * 1024

GATE_GROUP = 256
RC_HIST = SUBLANES
CC_HIST = 32
CONV_BLOCK = 4

BF16 = jnp.bfloat16
F32 = jnp.float32


def _sigmoid(x):
    return jax.nn.sigmoid(x)


def _silu(x):
    return x * _sigmoid(x)


def _gelu_tanh(x):
    c = 0.7978845608028654
    return 0.5 * x * (1.0 + jnp.tanh(c * (x + 0.044715 * (x * x * x))))


def _softplus(x):
    return jnp.maximum(x, 0.0) + jnp.log1p(jnp.exp(-jnp.abs(x)))


def _rmsnorm(x, g):
    ms = jnp.mean(x * x, axis=-1, keepdims=True)
    return x * lax.rsqrt(ms + EPS) * g


def _dot(a, b):
    return jnp.dot(a, b, preferred_element_type=F32)


def _resident(block_shape, index_map):
    return pl.BlockSpec(block_shape, index_map, pipeline_mode=pl.Buffered(1))


def _ada_kernel(c_ref, w_ref, b_ref, o_ref):
    s = _silu(c_ref[...]).astype(BF16)
    o_ref[...] = _dot(s, w_ref[...].astype(BF16)) + b_ref[...]


def _ada(c_all, ada_w, ada_b):
    depth, d, n = ada_w.shape
    nb = c_all.shape[0]
    tn = 1536
    return pl.pallas_call(
        _ada_kernel,
        grid=(depth, n // tn),
        in_specs=[
            pl.BlockSpec((nb, d), lambda l, j: (0, 0)),
            pl.BlockSpec((None, d, tn), lambda l, j: (l, 0, j)),
            pl.BlockSpec((None, 1, tn), lambda l, j: (l, 0, j)),
        ],
        out_specs=pl.BlockSpec((None, nb, tn), lambda l, j: (l, 0, j)),
        out_shape=jax.ShapeDtypeStruct((depth, nb, n), F32),
        compiler_params=pltpu.CompilerParams(
            dimension_semantics=("arbitrary", "arbitrary"),
            vmem_limit_bytes=VMEM_LIMIT_BYTES),
        name="ada_mod",
    )(c_all, ada_w, ada_b.reshape(depth, 1, n))


LANE_BLOCKS = D_MODEL // LANES


def _rows_to_tile_major(tm_ref, val, lane_block0=0, row_block0=0):
    nb = val.shape[0] // SUBLANES
    for j in range(val.shape[1] // LANES):
        lo = (lane_block0 + j) * SUBLANES
        tm_ref[row_block0:row_block0 + nb, lo:lo + SUBLANES, :] = (
            val[:, j * LANES:(j + 1) * LANES].reshape(nb, SUBLANES, LANES))


def _tile_major_to_rows(tm_ref, row_block0=0, n_blocks=None):
    nb = tm_ref.shape[0] if n_blocks is None else n_blocks
    return jnp.concatenate(
        [tm_ref[row_block0:row_block0 + nb, j * SUBLANES:(j + 1) * SUBLANES, :].reshape(nb * SUBLANES, LANES)
         for j in range(LANE_BLOCKS)], axis=1)


def _load_token(tm_ref, blk, sub):
    return tm_ref[blk, pl.ds(sub, SUBLANES, stride=SUBLANES), :]


def _store_token(tm_ref, blk, sub, val):
    tm_ref[blk, pl.ds(sub, SUBLANES, stride=SUBLANES), :] = val


def _mixer_kernel(x_ref, mod_ref, h0_ref, rc0_ref, cc0_ref,
                  n1g_ref, win_ref, rgw_ref, rgb_ref, gw_ref, ba_ref, bx_ref,
                  lam_ref, wa_ref, ccw_ref, ccb_ref, lng_ref, lnb_ref, wb_ref,
                  bb_ref, wo_ref,
                  xo_ref, ho_ref, rco_ref, cco_ref,
                  hn_s, xbuf, xc_s, a_tm, b_tm, u_tm, y_tm, vt, vtb, hcar, st_tm,
                  *, S, T, n_chunks):
    C = D_MODEL
    R = S * T
    HT = CC_HIST + T
    t_idx = pl.program_id(1)

    @pl.when(t_idx == 0)
    def _load_state():
        xbuf[:, RC_HIST - (RG_CONV_W - 1):RC_HIST, :] = rc0_ref[...]
        pad_rows = jnp.zeros((CC_HIST - (CC_CONV_W - 1), C), F32)
        for s in range(S):
            _rows_to_tile_major(st_tm, jnp.concatenate([cc0_ref[s], pad_rows], axis=0))
            for p in range(CC_CONV_W - 1):
                vt[s * HT + CC_HIST - (CC_CONV_W - 1) + p] = _load_token(st_tm, *divmod(p, SUBLANES))
            _rows_to_tile_major(st_tm, jnp.broadcast_to(h0_ref[s], (SUBLANES, C)))
            hcar[s] = _load_token(st_tm, 0, 0)

    x3 = x_ref[...]
    sh1 = mod_ref[:, 0:1, :]
    sc1 = mod_ref[:, 1:2, :]
    g1 = mod_ref[:, 2:3, :]
    hn = _rmsnorm(x3, n1g_ref[...]) * (1.0 + sc1) + sh1
    hn_s[...] = hn.reshape(R, C).astype(BF16)

    def proj(j):
        return _dot(hn_s[...], win_ref[:, j * C:(j + 1) * C])

    xbuf[:, RC_HIST:RC_HIST + T, :] = proj(0).reshape(S, T, C)
    xc3 = rgb_ref[...] + rgw_ref[0:1, :] * xbuf[:, RC_HIST - 3:RC_HIST - 3 + T, :]
    for k in range(1, RG_CONV_W):
        lo = RC_HIST - 3 + k
        xc3 = xc3 + rgw_ref[k:k + 1, :] * xbuf[:, lo:lo + T, :]
    xc_s[...] = xc3.reshape(R, C)

    for g in range(C // GATE_GROUP):
        cs = slice(g * GATE_GROUP, (g + 1) * GATE_GROUP)
        xg = xc_s[:, cs]
        pre = _dot(xg.astype(BF16), gw_ref[g])
        r = _sigmoid(pre[:, :GATE_GROUP] + ba_ref[:, cs])
        i = _sigmoid(pre[:, GATE_GROUP:] + bx_ref[:, cs])
        log_a = (-RG_C * _softplus(-lam_ref[:, cs])) * r
        a = jnp.exp(log_a)
        mult = jnp.sqrt(jnp.tanh(-log_a) * (a * a + 1.0))
        _rows_to_tile_major(a_tm, a, g * (GATE_GROUP // LANES))
        _rows_to_tile_major(b_tm, mult * (i * xg), g * (GATE_GROUP // LANES))

    for s in range(S):
        h = hcar[s]
        for t in range(T):
            blk, sub = divmod(s * T + t, SUBLANES)
            h = _load_token(a_tm, blk, sub) * h + _load_token(b_tm, blk, sub)
            _store_token(b_tm, blk, sub, h)
        hcar[s] = h

    ya_in = (_gelu_tanh(proj(1)) * _tile_major_to_rows(b_tm)).astype(BF16)
    ya = _dot(ya_in, wa_ref[...])

    _rows_to_tile_major(u_tm, proj(2) * _sigmoid(proj(3)))
    for s in range(S):
        for t in range(T):
            blk, sub = divmod(s * T + t, SUBLANES)
            vt[s * HT + CC_HIST + t] = _load_token(u_tm, blk, sub)

    n_streams, Lp = (1, T // 2) if S == 1 else (S // 2, T)
    hi_off = T // 2 if S == 1 else (S // 2) * HT
    HP = CC_HIST + Lp
    first_tap = CC_HIST - (CC_CONV_W - 1)
    for q in range(n_streams):
        for p in range(first_tap, HP):
            lo = q * HT + p
            vtb[q * HP + p] = jnp.concatenate([vt[lo], vt[lo + hi_off]], axis=0).astype(BF16)

    taps = [ccw_ref[k] for k in range(CC_CONV_W)]
    bias = ccb_ref[...]
    blocks_per_stream = Lp // SUBLANES

    def conv_rows(bi, carry):
        q = 0 if n_streams == 1 else bi // blocks_per_stream
        base = bi * SUBLANES + q * CC_HIST + first_tap
        for i0 in range(0, SUBLANES, CONV_BLOCK):
            acc = [None] * CONV_BLOCK
            for j in range(CONV_BLOCK + CC_CONV_W - 1):
                xj = vtb[base + i0 + j].astype(F32)
                for i in range(CONV_BLOCK):
                    k = j - i
                    if 0 <= k < CC_CONV_W:
                        prod = xj * taps[k].astype(F32)
                        acc[i] = prod if acc[i] is None else acc[i] + prod
            for i in range(CONV_BLOCK):
                _store_token(y_tm, bi, i0 + i, acc[i][:SUBLANES] + bias)
                _store_token(y_tm, bi + R // (2 * SUBLANES), i0 + i, acc[i][SUBLANES:] + bias)
        return carry

    lax.fori_loop(0, R // (2 * SUBLANES), conv_rows, 0)

    v = _tile_major_to_rows(y_tm)
    mu = jnp.mean(v, axis=-1, keepdims=True)
    vc = v - mu
    vn = vc * lax.rsqrt(jnp.mean(vc * vc, axis=-1, keepdims=True) + EPS)
    vn = _silu(vn * lng_ref[...] + lnb_ref[...])
    yb = _dot(vn.astype(BF16), wb_ref[...]) + bb_ref[...]

    m = _sigmoid(proj(4)) * ya + _sigmoid(proj(5)) * yb
    o = _dot(m.astype(BF16), wo_ref[...])
    xo_ref[...] = x3 + (1.0 + g1) * o.reshape(S, T, C)

    @pl.when(t_idx == n_chunks - 1)
    def _write_state():
        rco_ref[...] = xbuf[:, RC_HIST + T - (RG_CONV_W - 1):RC_HIST + T, :]
        for s in range(S):
            _store_token(st_tm, 0, 0, hcar[s])
            ho_ref[s] = _tile_major_to_rows(st_tm, 0, 1)[0:1, :]
            for p in range(CC_CONV_W - 1):
                _store_token(st_tm, *divmod(p, SUBLANES), vt[s * HT + HT - (CC_CONV_W - 1) + p])
            cco_ref[s] = _tile_major_to_rows(st_tm)[0:CC_CONV_W - 1, :]

    if n_chunks > 1:
        xbuf[:, 0:RC_HIST, :] = xbuf[:, T:T + RC_HIST, :]
        for s in range(S):
            vt[s * HT:s * HT + CC_HIST] = vt[s * HT + T:s * HT + T + CC_HIST]


N_MIXER_INPUTS = 21


def _mixer_kernel_with_aliases(*refs, n_alias, **static):
    _mixer_kernel(*refs[:N_MIXER_INPUTS], *refs[N_MIXER_INPUTS + n_alias:], **static)


def _mixer(l, x, mod, states, prev_states, w, *, S, T):
    h0, rc0, cc0 = states
    B, Tt, D = x.shape
    C = D
    n_chunks = Tt // T
    assert B % S == 0 and Tt % T == 0 and T % SUBLANES == 0
    assert S == 1 or n_chunks == 1
    assert n_chunks == 1 or T >= CC_HIST
    R = S * T

    def seq(*tail):
        return pl.BlockSpec((S,) + tail, lambda b, t: (b,) + (0,) * len(tail))

    def state(*tail):
        return pl.BlockSpec((None, S) + tail, lambda b, t: (l, b) + (0,) * len(tail))

    def layer(*tail):
        return _resident((None,) + tail, lambda b, t: (l,) + (0,) * len(tail))

    in_specs = [
        pl.BlockSpec((S, T, D), lambda b, t: (b, t, 0)),
        seq(6, D),
        state(1, C), state(RG_CONV_W - 1, C), state(CC_CONV_W - 1, C),
        layer(1, D),
        layer(D, 6 * C),
        layer(RG_CONV_W, C), layer(1, C),
        layer(C // GATE_GROUP, GATE_GROUP, 2 * GATE_GROUP),
        layer(1, C), layer(1, C), layer(1, C),
        layer(C, D),
        layer(CC_CONV_W, 2 * SUBLANES, LANES), layer(SUBLANES, LANES),
        layer(1, C), layer(1, C),
        layer(C, D), layer(1, D),
        layer(D, D),
    ]
    out_specs = [
        pl.BlockSpec((S, T, D), lambda b, t: (b, t, 0)),
        state(1, C), state(RG_CONV_W - 1, C), state(CC_CONV_W - 1, C),
    ]
    out_shape = [jax.ShapeDtypeStruct((B, Tt, D), F32)] + [
        jax.ShapeDtypeStruct(a.shape, F32) for a in states]
    aliases = {}
    alias_args = []
    if prev_states is not None:
        alias_args = list(prev_states)
        in_specs += [pl.BlockSpec(memory_space=pl.ANY)] * len(alias_args)
        aliases = {N_MIXER_INPUTS + k: 1 + k for k in range(len(alias_args))}
    tile_major = pltpu.VMEM((R // SUBLANES, LANE_BLOCKS * SUBLANES, LANES), F32)
    scratch = [
        pltpu.VMEM((R, D), BF16),
        pltpu.VMEM((S, RC_HIST + T, C), F32),
        pltpu.VMEM((R, C), F32),
        tile_major, tile_major, tile_major, tile_major,
        pltpu.VMEM((S * (CC_HIST + T), SUBLANES, LANES), F32),
        pltpu.VMEM((R // 2 + CC_HIST * (1 if S == 1 else S // 2), 2 * SUBLANES, LANES), BF16),
        pltpu.VMEM((S, SUBLANES, LANES), F32),
        pltpu.VMEM((CC_HIST // SUBLANES, LANE_BLOCKS * SUBLANES, LANES), F32),
    ]
    args = [x, mod, h0, rc0, cc0,
            w["norm1_g"], w["w_in"], w["rg_conv_w"], w["rg_conv_b"], w["gate_w"],
            w["rg_ba"], w["rg_bx"], w["rg_lambda"], w["w_branch_a"],
            w["cc_dw_w"], w["cc_dw_b"], w["cc_ln_g"], w["cc_ln_b"], w["w_branch_b"],
            w["b_branch_b"], w["w_out"]]
    assert len(args) == N_MIXER_INPUTS
    outs = pl.pallas_call(
        functools.partial(_mixer_kernel_with_aliases, n_alias=len(alias_args), S=S, T=T, n_chunks=n_chunks),
        grid=(B // S, n_chunks),
        in_specs=in_specs,
        out_specs=out_specs,
        out_shape=out_shape,
        scratch_shapes=scratch,
        input_output_aliases=aliases,
        compiler_params=pltpu.CompilerParams(
            dimension_semantics=("arbitrary", "arbitrary"),
            vmem_limit_bytes=VMEM_LIMIT_BYTES),
        name=f"mixer_S{S}",
    )(*args, *alias_args)
    return outs[0], tuple(outs[1:])


def _finish(x3, g2, f, fg_ref, o_ref):
    y = x3 + (1.0 + g2) * f.reshape(x3.shape)
    if fg_ref is not None:
        y = _rmsnorm(y, fg_ref[...])
    o_ref[...] = y


def _ffn_kernel(x_ref, mod_ref, n2g_ref, wg_ref, wu_ref, wd_ref, *rest, n_split, final):
    fg_ref = rest[0] if final else None
    o_ref = rest[-1]
    x3 = x_ref[...]
    S, T, D = x3.shape
    sh2 = mod_ref[:, 3:4, :]
    sc2 = mod_ref[:, 4:5, :]
    g2 = mod_ref[:, 5:6, :]
    hn = (_rmsnorm(x3, n2g_ref[...]) * (1.0 + sc2) + sh2).reshape(S * T, D).astype(BF16)
    ff = wg_ref.shape[1] // n_split
    f = None
    for j in range(n_split):
        cs = slice(j * ff, (j + 1) * ff)
        h = (_silu(_dot(hn, wg_ref[:, cs])) * _dot(hn, wu_ref[:, cs])).astype(BF16)
        part = _dot(h, wd_ref[cs, :])
        f = part if f is None else f + part
    _finish(x3, g2, f, fg_ref, o_ref)


def _ffn(l, j, x, mod, w, final_g, *, S, T):
    B, Tt, D = x.shape
    ff = w["ffn_w_gate"].shape[-1]
    final = final_g is not None
    in_specs = [
        pl.BlockSpec((S, T, D), lambda b, t: (b, t, 0)),
        pl.BlockSpec((S, 6, D), lambda b, t: (b, 0, 0)),
        _resident((None, 1, D), lambda b, t: (l, 0, 0)),
        _resident((None, D, ff), lambda b, t: (j, 0, 0)),
        _resident((None, D, ff), lambda b, t: (j, 0, 0)),
        _resident((None, ff, D), lambda b, t: (j, 0, 0)),
    ]
    args = [x, mod, w["norm2_g"], w["ffn_w_gate"], w["ffn_w_up"], w["ffn_w_down"]]
    if final:
        in_specs.append(_resident((1, D), lambda b, t: (0, 0)))
        args.append(final_g)
    return pl.pallas_call(
        functools.partial(_ffn_kernel, n_split=2, final=final),
        grid=(B // S, Tt // T),
        in_specs=in_specs,
        out_specs=pl.BlockSpec((S, T, D), lambda b, t: (b, t, 0)),
        out_shape=jax.ShapeDtypeStruct((B, Tt, D), F32),
        compiler_params=pltpu.CompilerParams(
            dimension_semantics=("arbitrary", "arbitrary"),
            vmem_limit_bytes=VMEM_LIMIT_BYTES),
        name=f"ffn_S{S}",
    )(*args)


MOE_BLOCK = 1024
MOE_ROWS = 128
MOE_SLOTS = TOP_K * MOE_BLOCK + N_EXPERTS * MOE_ROWS


MOE_FINISH_ROWS = 256


def _adaln2(x3, mod_ref, n2g_ref, rows):
    return _rmsnorm(x3, n2g_ref[...]) * (1.0 + mod_ref[rows, 4:5, :]) + mod_ref[rows, 3:4, :]


def _route_kernel(x_ref, mod_ref, n2g_ref, wr_ref, br_ref, meta_ref, info_ref):
    x3 = x_ref[...]
    S, T, D = x3.shape
    R = S * T
    hn = _adaln2(x3, mod_ref, n2g_ref, slice(None)).reshape(R, D)
    hn_hi = hn.astype(BF16)
    hn_lo = (hn - hn_hi.astype(F32)).astype(BF16)
    both = _dot(hn_hi, wr_ref[...])
    logits = both[:, :LANES] + both[:, LANES:] + _dot(hn_lo, wr_ref[:, :LANES]) + br_ref[...]
    z = logits - jnp.max(logits, axis=-1, keepdims=True)
    p = jnp.exp(z)
    p = p / jnp.sum(p, axis=-1, keepdims=True)
    lane_i = lax.broadcasted_iota(jnp.int32, p.shape, 1)
    lane = lane_i.astype(F32)
    far = jnp.float32(p.shape[1])
    m1 = jnp.max(p, axis=-1, keepdims=True)
    i1 = jnp.min(jnp.where(p == m1, lane, far), axis=-1, keepdims=True)
    pick1 = lane == i1
    p2 = jnp.where(pick1, -1.0, p)
    m2 = jnp.max(p2, axis=-1, keepdims=True)
    i2 = jnp.min(jnp.where(p2 == m2, lane, far), axis=-1, keepdims=True)
    pick2 = lane == i2
    onehot = jnp.where(pick1, 1.0, 0.0) + jnp.where(pick2, 1.0, 0.0)
    earlier = (lax.broadcasted_iota(jnp.int32, (R, R), 1) < lax.broadcasted_iota(jnp.int32, (R, R), 0))
    rank = _dot(jnp.where(earlier, 1.0, 0.0).astype(BF16), onehot.astype(BF16))
    count = rank[R - 1:R, :] + onehot[R - 1:R, :]
    count8 = jnp.broadcast_to(count, (SUBLANES, LANES))
    padded = jnp.floor((count8 + (MOE_ROWS - 1)) * (1.0 / MOE_ROWS)) * MOE_ROWS
    incl = padded
    lane8 = lax.broadcasted_iota(jnp.int32, (SUBLANES, LANES), 1)
    for k in (1, 2, 4):
        incl = incl + jnp.where(lane8 >= k, pltpu.roll(incl, k, 1), 0.0)
    start8 = incl - padded
    slot = start8[0:1, :] + rank
    slot1 = jnp.sum(jnp.where(pick1, slot, 0.0), axis=-1, keepdims=True)
    slot2 = jnp.sum(jnp.where(pick2, slot, 0.0), axis=-1, keepdims=True)
    den = m1 + m2
    meta_ref[...] = jnp.where(lane_i == 0, slot1, jnp.where(lane_i == 1, slot2,
                              jnp.where(lane_i == 2, m1 / den, jnp.where(lane_i == 3, m2 / den, 0.0))))
    sub8 = lax.broadcasted_iota(jnp.int32, (SUBLANES, LANES), 0)
    info_ref[...] = jnp.where(sub8 == 0, start8, jnp.where(sub8 == 1, count8, 0.0))


def _route(l, j, x, mod, w, *, S, T):
    B, Tt, D = x.shape
    R = S * T
    assert R == MOE_BLOCK
    nt = Tt // T
    nblk = (B // S) * nt
    return pl.pallas_call(
        _route_kernel,
        grid=(B // S, nt),
        in_specs=[
            pl.BlockSpec((S, T, D), lambda b, t: (b, t, 0)),
            pl.BlockSpec((S, 6, D), lambda b, t: (b, 0, 0)),
            _resident((None, 1, D), lambda b, t: (l, 0, 0)),
            _resident((None, D, 2 * LANES), lambda b, t: (j, 0, 0)),
            _resident((None, 1, LANES), lambda b, t: (j, 0, 0)),
        ],
        out_specs=[
            pl.BlockSpec((None, R, LANES), lambda b, t: (b * nt + t, 0, 0)),
            pl.BlockSpec((None, SUBLANES, LANES), lambda b, t: (b * nt + t, 0, 0)),
        ],
        out_shape=[
            jax.ShapeDtypeStruct((nblk, R, LANES), F32),
            jax.ShapeDtypeStruct((nblk, SUBLANES, LANES), F32),
        ],
        compiler_params=pltpu.CompilerParams(
            dimension_semantics=("arbitrary", "arbitrary"),
            vmem_limit_bytes=VMEM_LIMIT_BYTES),
        name=f"route_S{S}",
    )(x, mod, w["norm2_g"], w["router_w"], w["router_b"])


def _row_chunks(S, T, rows):
    if S == 1:
        return [(slice(0, 1), slice(r0, r0 + rows), r0) for r0 in range(0, T, rows)]
    per = rows // T
    return [(slice(s0, s0 + per), slice(None), s0 * T) for s0 in range(0, S, per)]


def _moe_kernel(start_ref, count_ref, x_ref, mod_ref, n2g_ref, slot_ref, gate_ref, wg_ref, wu_ref, wd_ref,
                *rest, final):
    fg_ref = rest[0] if final else None
    o_ref, hn_tm, xs_tm, y_tm, ys, inv = rest[-6:]
    b = pl.program_id(0)
    e = pl.program_id(1)
    S, T, D = x_ref.shape
    TB = S * T
    pieces = _row_chunks(S, T, MOE_FINISH_ROWS)

    @pl.when(e == 0)
    def _prepare():
        for ss, ts, r0 in pieces:
            hn = _adaln2(x_ref[ss, ts, :], mod_ref, n2g_ref, ss)
            _rows_to_tile_major(hn_tm, hn.reshape(MOE_FINISH_ROWS, D), 0, r0 // SUBLANES)

        def clear(i, c):
            inv[i] = 0
            return c
        for ex in range(N_EXPERTS):
            cnt = count_ref[b * N_EXPERTS + ex]
            lo = start_ref[b * N_EXPERTS + ex] + cnt
            lax.fori_loop(lo, lo + lax.rem(MOE_ROWS - lax.rem(cnt, MOE_ROWS), MOE_ROWS), clear, 0)

        def fill(i, c):
            inv[slot_ref[0, TOP_K * i]] = i
            inv[slot_ref[0, TOP_K * i + 1]] = i
            return c
        lax.fori_loop(0, TB, fill, 0, unroll=SUBLANES)

    first = start_ref[b * N_EXPERTS + e]
    n_rows = count_ref[b * N_EXPERTS + e]
    n_chunks = lax.div(n_rows + (MOE_ROWS - 1), MOE_ROWS)

    last = N_EXPERTS - 1
    last_base = (start_ref[b * N_EXPERTS + last]
                 + lax.div(count_ref[b * N_EXPERTS + last] + (MOE_ROWS - 1), MOE_ROWS) * MOE_ROWS - MOE_ROWS)

    def gather(base):
        dst = xs_tm.at[lax.bitwise_and(lax.div(base, MOE_ROWS), 1)]
        for r in range(MOE_ROWS):
            t = inv[base + r]
            _store_token(dst, r // SUBLANES, r % SUBLANES,
                         _load_token(hn_tm, lax.shift_right_logical(t, 3), lax.bitwise_and(t, 7)))

    @pl.when(e == 0)
    def _first_gather():
        gather(0)

    def chunk(c, carry):
        base = first + c * MOE_ROWS
        lhs = _tile_major_to_rows(xs_tm.at[lax.bitwise_and(lax.div(base, MOE_ROWS), 1)]).astype(BF16)
        gather(jnp.minimum(base + MOE_ROWS, last_base))
        h = (_silu(_dot(lhs, wg_ref[...])) * _dot(lhs, wu_ref[...])).astype(BF16)
        _rows_to_tile_major(y_tm, _dot(h, wd_ref[...]))
        for r in range(MOE_ROWS):
            ys[base + r] = _load_token(y_tm, r // SUBLANES, r % SUBLANES)
        return carry

    lax.fori_loop(0, n_chunks, chunk, 0)

    @pl.when(e == N_EXPERTS - 1)
    def _combine():
        def comb(i8, c):
            for i in range(SUBLANES):
                t = i8 * SUBLANES + i
                f = (gate_ref[0, TOP_K * t] * ys[slot_ref[0, TOP_K * t]]
                     + gate_ref[0, TOP_K * t + 1] * ys[slot_ref[0, TOP_K * t + 1]])
                _store_token(hn_tm, i8, i, f)
            return c
        lax.fori_loop(0, TB // SUBLANES, comb, 0)
        for ss, ts, r0 in pieces:
            x3 = x_ref[ss, ts, :]
            f = _tile_major_to_rows(hn_tm, r0 // SUBLANES, MOE_FINISH_ROWS // SUBLANES)
            y = x3 + (1.0 + mod_ref[ss, 5:6, :]) * f.reshape(x3.shape)
            if final:
                y = _rmsnorm(y, fg_ref[...])
            o_ref[ss, ts, :] = y


def _moe(l, j, x, mod, w, final_g, *, S, T):
    B, Tt, D = x.shape
    ff = w["moe_w_gate"].shape[-1]
    final = final_g is not None
    TB = S * T
    nt = Tt // T
    nblk = (B // S) * nt
    meta, info = _route(l, j, x, mod, w, S=S, T=T)

    slots = meta[:, :, 0:TOP_K].astype(jnp.int32).reshape(nblk, 1, TOP_K * TB)
    gates = meta[:, :, TOP_K:2 * TOP_K].reshape(nblk, 1, TOP_K * TB)
    starts = info[:, 0, :N_EXPERTS].astype(jnp.int32).reshape(nblk * N_EXPERTS)
    counts = info[:, 1, :N_EXPERTS].astype(jnp.int32).reshape(nblk * N_EXPERTS)

    smem_row = lambda: pl.BlockSpec((None, 1, TOP_K * TB), lambda b, e, st, ct: (b, 0, 0), memory_space=pltpu.SMEM)
    in_specs = [
        pl.BlockSpec((S, T, D), lambda b, e, st, ct: (b // nt, b % nt, 0), pipeline_mode=pl.Buffered(1)),
        pl.BlockSpec((S, 6, D), lambda b, e, st, ct: (b // nt, 0, 0)),
        pl.BlockSpec((None, 1, D), lambda b, e, st, ct: (l, 0, 0)),
        smem_row(), smem_row(),
        pl.BlockSpec((None, None, D, ff), lambda b, e, st, ct: (j, e, 0, 0)),
        pl.BlockSpec((None, None, D, ff), lambda b, e, st, ct: (j, e, 0, 0)),
        pl.BlockSpec((None, None, ff, D), lambda b, e, st, ct: (j, e, 0, 0)),
    ]
    args = [x, mod, w["norm2_g"], slots, gates, w["moe_w_gate"], w["moe_w_up"], w["moe_w_down"]]
    if final:
        in_specs.append(pl.BlockSpec((1, D), lambda b, e, st, ct: (0, 0)))
        args.append(final_g)
    tile_major = lambda rows: pltpu.VMEM((rows // SUBLANES, LANE_BLOCKS * SUBLANES, LANES), F32)
    return pl.pallas_call(
        functools.partial(_moe_kernel, final=final),
        grid_spec=pltpu.PrefetchScalarGridSpec(
            num_scalar_prefetch=2,
            grid=(nblk, N_EXPERTS),
            in_specs=in_specs,
            out_specs=pl.BlockSpec((S, T, D), lambda b, e, st, ct: (b // nt, b % nt, 0)),
            scratch_shapes=[
                tile_major(TB),
                pltpu.VMEM((2, MOE_ROWS // SUBLANES, LANE_BLOCKS * SUBLANES, LANES), F32),
                tile_major(MOE_ROWS),
                pltpu.VMEM((MOE_SLOTS, SUBLANES, LANES), F32),
                pltpu.SMEM((MOE_SLOTS,), jnp.int32),
            ]),
        out_shape=jax.ShapeDtypeStruct((B, Tt, D), F32),
        compiler_params=pltpu.CompilerParams(
            dimension_semantics=("arbitrary", "arbitrary"),
            vmem_limit_bytes=VMEM_LIMIT_BYTES),
        name=f"moe_S{S}",
    )(starts, counts, *args)


def _block_diag_gate_weights(wa, wx):
    def bd(w):
        L, H, bs, _ = w.shape
        hg = GATE_GROUP // bs
        w5 = w.reshape(L, H // hg, hg, bs, bs)
        eye = jnp.eye(hg, dtype=w.dtype)
        m = jnp.einsum("lqhij,hk->lqhikj", w5, eye)
        return m.reshape(L, H // hg, GATE_GROUP, GATE_GROUP)
    return jnp.concatenate([bd(wa), bd(wx)], axis=-1)


def kernel(x_prompt, x_sample, c_prompt, c_sample, state_rglru_h, state_rglru_conv, state_cconv,
           ada_w, ada_b, norm1_g, norm2_g, w_in, rg_conv_w, rg_conv_b, rg_wa, rg_ba, rg_wx, rg_bx,
           rg_lambda, w_branch_a, cc_dw_w, cc_dw_b, cc_ln_g, cc_ln_b, w_branch_b, b_branch_b, w_out,
           ffn_w_gate, ffn_w_up, ffn_w_down, moe_router_w, moe_router_b, moe_w_gate, moe_w_up,
           moe_w_down, final_norm_g):
    depth = ada_w.shape[0]
    D = D_MODEL
    row = lambda a: a.reshape(a.shape[0], 1, a.shape[-1])
    n_moe = moe_router_w.shape[0]
    router_w = jnp.zeros((n_moe, D, LANES), F32).at[:, :, :N_EXPERTS].set(moe_router_w)
    router_hi = router_w.astype(BF16)
    router_lo = (router_w - router_hi.astype(F32)).astype(BF16)
    router_w = jnp.concatenate([router_hi, router_lo], axis=-1)
    router_b = jnp.full((n_moe, 1, LANES), -1e30, F32).at[:, 0, :N_EXPERTS].set(moe_router_b)
    w = {
        "norm1_g": row(norm1_g), "norm2_g": row(norm2_g),
        "w_in": w_in.astype(BF16),
        "rg_conv_w": rg_conv_w, "rg_conv_b": row(rg_conv_b),
        "gate_w": _block_diag_gate_weights(rg_wa, rg_wx).astype(BF16),
        "rg_ba": row(rg_ba), "rg_bx": row(rg_bx), "rg_lambda": row(rg_lambda),
        "w_branch_a": w_branch_a.astype(BF16),
        "cc_dw_w": jnp.tile(cc_dw_w.reshape(depth, CC_CONV_W, SUBLANES, LANES), (1, 1, 2, 1)).astype(BF16),
        "cc_dw_b": cc_dw_b.reshape(depth, SUBLANES, LANES),
        "cc_ln_g": row(cc_ln_g), "cc_ln_b": row(cc_ln_b),
        "w_branch_b": w_branch_b.astype(BF16), "b_branch_b": row(b_branch_b),
        "w_out": w_out.astype(BF16),
        "ffn_w_gate": ffn_w_gate.astype(BF16), "ffn_w_up": ffn_w_up.astype(BF16),
        "ffn_w_down": ffn_w_down.astype(BF16),
        "router_w": router_w, "router_b": router_b,
        "moe_w_gate": moe_w_gate.astype(BF16), "moe_w_up": moe_w_up.astype(BF16),
        "moe_w_down": moe_w_down.astype(BF16),
    }
    final_g = final_norm_g.reshape(1, D)

    nb_p = x_prompt.shape[0]
    nb_s = x_sample.shape[0]
    mod_all = _ada(jnp.concatenate([c_prompt, c_sample], axis=0), ada_w, ada_b)
    mod_all = mod_all.reshape(depth, nb_p + nb_s, 6, D)

    dt = x_prompt.dtype
    zeros_p = (jnp.zeros((depth, nb_p, 1, D), dt),
               jnp.zeros((depth, nb_p, RG_CONV_W - 1, D), dt),
               jnp.zeros((depth, nb_p, CC_CONV_W - 1, D), dt))
    state_s = (state_rglru_h.reshape(depth, nb_s, 1, D), state_rglru_conv, state_cconv)

    def trunk(x, mod, states, *, mix_tile, ffn_tile, moe_tile):
        new_states = None
        for l in range(depth):
            x, new_states = _mixer(l, x, mod[l], states, new_states, w, S=mix_tile[0], T=mix_tile[1])
            fg = final_g if l == depth - 1 else None
            if l % 2 == 0:
                x = _ffn(l, l // 2, x, mod[l], w, fg, S=ffn_tile[0], T=ffn_tile[1])
            else:
                x = _moe(l, l // 2, x, mod[l], w, fg, S=moe_tile[0], T=moe_tile[1])
        h, rc, cc = new_states
        return x, h[:, :, 0, :], rc, cc

    y_p, h_p, rc_p, cc_p = trunk(x_prompt, mod_all[:, :nb_p], zeros_p,
                                 mix_tile=(1, 256), ffn_tile=(1, 512), moe_tile=(1, MOE_BLOCK))
    y_s, h_s, rc_s, cc_s = trunk(x_sample, mod_all[:, nb_p:], state_s,
                                 mix_tile=(16, 8), ffn_tile=(64, 8), moe_tile=(MOE_BLOCK // 8, 8))
    return (y_p, y_s, h_p, rc_p, cc_p, h_s, rc_s, cc_s)
```

```python
import functools

import jax
import jax.numpy as jnp
from jax import lax
from jax.experimental import pallas as pl
from jax.experimental.pallas import tpu as pltpu

D_MODEL = 1024
N_RNN_HEADS = 16
RG_CONV_W = 4
RG_C = 8.0
CC_CONV_W = 31
N_EXPERTS = 8
TOP_K = 2
EPS = 1e-6

LANES = 128
SUBLANES = 8
VMEM_LIMIT_BYTES = 56 * 1024 * 1024

GATE_GROUP = 256
RC_HIST = SUBLANES
CC_HIST = 32
CONV_BLOCK = 4

BF16 = jnp.bfloat16
F32 = jnp.float32


def _sigmoid(x):
    return jax.nn.sigmoid(x)


def _silu(x):
    return x * _sigmoid(x)


def _gelu_tanh(x):
    c = 0.7978845608028654
    return 0.5 * x * (1.0 + jnp.tanh(c * (x + 0.044715 * (x * x * x))))


def _softplus(x):
    return jnp.maximum(x, 0.0) + jnp.log1p(jnp.exp(-jnp.abs(x)))


def _rmsnorm(x, g):
    ms = jnp.mean(x * x, axis=-1, keepdims=True)
    return x * lax.rsqrt(ms + EPS) * g


def _dot(a, b):
    return jnp.dot(a, b, preferred_element_type=F32)


def _resident(block_shape, index_map):
    return pl.BlockSpec(block_shape, index_map, pipeline_mode=pl.Buffered(1))


def _ada_kernel(c_ref, w_ref, b_ref, o_ref):
    s = _silu(c_ref[...]).astype(BF16)
    o_ref[...] = _dot(s, w_ref[...].astype(BF16)) + b_ref[...]


def _ada(c_all, ada_w, ada_b):
    depth, d, n = ada_w.shape
    nb = c_all.shape[0]
    tn = 1536
    return pl.pallas_call(
        _ada_kernel,
        grid=(depth, n // tn),
        in_specs=[
            pl.BlockSpec((nb, d), lambda l, j: (0, 0)),
            pl.BlockSpec((None, d, tn), lambda l, j: (l, 0, j)),
            pl.BlockSpec((None, 1, tn), lambda l, j: (l, 0, j)),
        ],
        out_specs=pl.BlockSpec((None, nb, tn), lambda l, j: (l, 0, j)),
        out_shape=jax.ShapeDtypeStruct((depth, nb, n), F32),
        compiler_params=pltpu.CompilerParams(
            dimension_semantics=("arbitrary", "arbitrary"),
            vmem_limit_bytes=VMEM_LIMIT_BYTES),
        name="ada_mod",
    )(c_all, ada_w, ada_b.reshape(depth, 1, n))


LANE_BLOCKS = D_MODEL // LANES


def _rows_to_tile_major(tm_ref, val, lane_block0=0, row_block0=0):
    nb = val.shape[0] // SUBLANES
    for j in range(val.shape[1] // LANES):
        lo = (lane_block0 + j) * SUBLANES
        tm_ref[row_block0:row_block0 + nb, lo:lo + SUBLANES, :] = (
            val[:, j * LANES:(j + 1) * LANES].reshape(nb, SUBLANES, LANES))


def _tile_major_to_rows(tm_ref, row_block0=0, n_blocks=None):
    nb = tm_ref.shape[0] if n_blocks is None else n_blocks
    return jnp.concatenate(
        [tm_ref[row_block0:row_block0 + nb, j * SUBLANES:(j + 1) * SUBLANES, :].reshape(nb * SUBLANES, LANES)
         for j in range(LANE_BLOCKS)], axis=1)


def _load_token(tm_ref, blk, sub):
    return tm_ref[blk, pl.ds(sub, SUBLANES, stride=SUBLANES), :]


def _store_token(tm_ref, blk, sub, val):
    tm_ref[blk, pl.ds(sub, SUBLANES, stride=SUBLANES), :] = val


def _mixer_kernel(x_ref, mod_ref, h0_ref, rc0_ref, cc0_ref,
                  n1g_ref, win_ref, rgw_ref, rgb_ref, gw_ref, ba_ref, bx_ref,
                  lam_ref, wa_ref, ccw_ref, ccb_ref, lng_ref, lnb_ref, wb_ref,
                  bb_ref, wo_ref,
                  xo_ref, ho_ref, rco_ref, cco_ref,
                  hn_s, xbuf, xc_s, a_tm, b_tm, u_tm, y_tm, vt, vtb, hcar, st_tm,
                  *, S, T, n_chunks, cc_time_major):
    C = D_MODEL
    R = S * T
    HT = CC_HIST + T
    t_idx = pl.program_id(1)

    @pl.when(t_idx == 0)
    def _load_state():
        xbuf[:, RC_HIST - (RG_CONV_W - 1):RC_HIST, :] = rc0_ref[...]
        hist0 = CC_HIST - (CC_CONV_W - 1)
        if cc_time_major:
            for p in range(CC_CONV_W - 1):
                _rows_to_tile_major(st_tm, cc0_ref[p])
                for s in range(S):
                    vt[s * HT + hist0 + p] = _load_token(st_tm, *divmod(s, SUBLANES))
        else:
            pad_rows = jnp.zeros((hist0, C), F32)
            for s in range(S):
                _rows_to_tile_major(st_tm, jnp.concatenate([cc0_ref[s], pad_rows], axis=0))
                for p in range(CC_CONV_W - 1):
                    vt[s * HT + hist0 + p] = _load_token(st_tm, *divmod(p, SUBLANES))
        for s in range(S):
            _rows_to_tile_major(st_tm, jnp.broadcast_to(h0_ref[s], (SUBLANES, C)))
            hcar[s] = _load_token(st_tm, 0, 0)

    x3 = x_ref[...]
    sh1 = mod_ref[:, 0:1, :]
    sc1 = mod_ref[:, 1:2, :]
    g1 = mod_ref[:, 2:3, :]
    hn = _rmsnorm(x3, n1g_ref[...]) * (1.0 + sc1) + sh1
    hn_s[...] = hn.reshape(R, C).astype(BF16)

    def proj(j):
        return _dot(hn_s[...], win_ref[:, j * C:(j + 1) * C])

    xbuf[:, RC_HIST:RC_HIST + T, :] = proj(0).reshape(S, T, C)
    xc3 = rgb_ref[...] + rgw_ref[0:1, :] * xbuf[:, RC_HIST - 3:RC_HIST - 3 + T, :]
    for k in range(1, RG_CONV_W):
        lo = RC_HIST - 3 + k
        xc3 = xc3 + rgw_ref[k:k + 1, :] * xbuf[:, lo:lo + T, :]
    xc_s[...] = xc3.reshape(R, C)

    for g in range(C // GATE_GROUP):
        cs = slice(g * GATE_GROUP, (g + 1) * GATE_GROUP)
        xg = xc_s[:, cs]
        pre = _dot(xg.astype(BF16), gw_ref[g])
        r = _sigmoid(pre[:, :GATE_GROUP] + ba_ref[:, cs])
        i = _sigmoid(pre[:, GATE_GROUP:] + bx_ref[:, cs])
        log_a = (-RG_C * _softplus(-lam_ref[:, cs])) * r
        a = jnp.exp(log_a)
        mult = jnp.sqrt(jnp.tanh(-log_a) * (a * a + 1.0))
        _rows_to_tile_major(a_tm, a, g * (GATE_GROUP // LANES))
        _rows_to_tile_major(b_tm, mult * (i * xg), g * (GATE_GROUP // LANES))

    for s in range(S):
        h = hcar[s]
        for t in range(T):
            blk, sub = divmod(s * T + t, SUBLANES)
            h = _load_token(a_tm, blk, sub) * h + _load_token(b_tm, blk, sub)
            _store_token(b_tm, blk, sub, h)
        hcar[s] = h

    ya_in = (_gelu_tanh(proj(1)) * _tile_major_to_rows(b_tm)).astype(BF16)
    ya = _dot(ya_in, wa_ref[...])

    _rows_to_tile_major(u_tm, proj(2) * _sigmoid(proj(3)))
    for s in range(S):
        for t in range(T):
            blk, sub = divmod(s * T + t, SUBLANES)
            vt[s * HT + CC_HIST + t] = _load_token(u_tm, blk, sub)

    n_streams, Lp = (1, T // 2) if S == 1 else (S // 2, T)
    hi_off = T // 2 if S == 1 else (S // 2) * HT
    HP = CC_HIST + Lp
    first_tap = CC_HIST - (CC_CONV_W - 1)
    for q in range(n_streams):
        for p in range(first_tap, HP):
            lo = q * HT + p
            vtb[q * HP + p] = jnp.concatenate([vt[lo], vt[lo + hi_off]], axis=0).astype(BF16)

    taps = [ccw_ref[k] for k in range(CC_CONV_W)]
    bias = ccb_ref[...]
    blocks_per_stream = Lp // SUBLANES

    def conv_rows(bi, carry):
        q = 0 if n_streams == 1 else bi // blocks_per_stream
        base = bi * SUBLANES + q * CC_HIST + first_tap
        for i0 in range(0, SUBLANES, CONV_BLOCK):
            acc = [None] * CONV_BLOCK
            for j in range(CONV_BLOCK + CC_CONV_W - 1):
                xj = vtb[base + i0 + j].astype(F32)
                for i in range(CONV_BLOCK):
                    k = j - i
                    if 0 <= k < CC_CONV_W:
                        prod = xj * taps[k].astype(F32)
                        acc[i] = prod if acc[i] is None else acc[i] + prod
            for i in range(CONV_BLOCK):
                _store_token(y_tm, bi, i0 + i, acc[i][:SUBLANES] + bias)
                _store_token(y_tm, bi + R // (2 * SUBLANES), i0 + i, acc[i][SUBLANES:] + bias)
        return carry

    lax.fori_loop(0, R // (2 * SUBLANES), conv_rows, 0)

    v = _tile_major_to_rows(y_tm)
    mu = jnp.mean(v, axis=-1, keepdims=True)
    vc = v - mu
    vn = vc * lax.rsqrt(jnp.mean(vc * vc, axis=-1, keepdims=True) + EPS)
    vn = _silu(vn * lng_ref[...] + lnb_ref[...])
    yb = _dot(vn.astype(BF16), wb_ref[...]) + bb_ref[...]

    m = _sigmoid(proj(4)) * ya + _sigmoid(proj(5)) * yb
    o = _dot(m.astype(BF16), wo_ref[...])
    xo_ref[...] = x3 + (1.0 + g1) * o.reshape(S, T, C)

    @pl.when(t_idx == n_chunks - 1)
    def _write_state():
        rco_ref[...] = xbuf[:, RC_HIST + T - (RG_CONV_W - 1):RC_HIST + T, :]
        for s in range(S):
            _store_token(st_tm, 0, 0, hcar[s])
            ho_ref[s] = _tile_major_to_rows(st_tm, 0, 1)[0:1, :]
        if cc_time_major:
            for p in range(CC_CONV_W - 1):
                for s in range(S):
                    _store_token(st_tm, *divmod(s, SUBLANES), vt[s * HT + HT - (CC_CONV_W - 1) + p])
                cco_ref[p] = _tile_major_to_rows(st_tm, 0, S // SUBLANES)
        else:
            for s in range(S):
                for p in range(CC_CONV_W - 1):
                    _store_token(st_tm, *divmod(p, SUBLANES), vt[s * HT + HT - (CC_CONV_W - 1) + p])
                cco_ref[s] = _tile_major_to_rows(st_tm)[0:CC_CONV_W - 1, :]

    if n_chunks > 1:
        xbuf[:, 0:RC_HIST, :] = xbuf[:, T:T + RC_HIST, :]
        for s in range(S):
            vt[s * HT:s * HT + CC_HIST] = vt[s * HT + T:s * HT + T + CC_HIST]


N_MIXER_INPUTS = 21


def _mixer_kernel_with_aliases(*refs, n_alias, **static):
    _mixer_kernel(*refs[:N_MIXER_INPUTS], *refs[N_MIXER_INPUTS + n_alias:], **static)


def _mixer(l, x, mod, states, prev_states, w, *, S, T, cc_time_major=False):
    h0, rc0, cc0 = states
    assert cc0.shape[1:3] == ((CC_CONV_W - 1, x.shape[0]) if cc_time_major else (x.shape[0], CC_CONV_W - 1))
    B, Tt, D = x.shape
    C = D
    n_chunks = Tt // T
    assert B % S == 0 and Tt % T == 0 and T % SUBLANES == 0
    assert S == 1 or n_chunks == 1
    assert n_chunks == 1 or T >= CC_HIST
    R = S * T

    def seq(*tail):
        return pl.BlockSpec((S,) + tail, lambda b, t: (b,) + (0,) * len(tail))

    def state(*tail):
        return pl.BlockSpec((None, S) + tail, lambda b, t: (l, b) + (0,) * len(tail))

    if cc_time_major:
        assert S % SUBLANES == 0 and S // SUBLANES <= CC_HIST // SUBLANES
        cc_spec = pl.BlockSpec((None, CC_CONV_W - 1, S, C), lambda b, t: (l, 0, b, 0))
    else:
        cc_spec = state(CC_CONV_W - 1, C)

    def layer(*tail):
        return _resident((None,) + tail, lambda b, t: (l,) + (0,) * len(tail))

    in_specs = [
        pl.BlockSpec((S, T, D), lambda b, t: (b, t, 0)),
        seq(6, D),
        state(1, C), state(RG_CONV_W - 1, C), cc_spec,
        layer(1, D),
        layer(D, 6 * C),
        layer(RG_CONV_W, C), layer(1, C),
        layer(C // GATE_GROUP, GATE_GROUP, 2 * GATE_GROUP),
        layer(1, C), layer(1, C), layer(1, C),
        layer(C, D),
        layer(CC_CONV_W, 2 * SUBLANES, LANES), layer(SUBLANES, LANES),
        layer(1, C), layer(1, C),
        layer(C, D), layer(1, D),
        layer(D, D),
    ]
    out_specs = [
        pl.BlockSpec((S, T, D), lambda b, t: (b, t, 0)),
        state(1, C), state(RG_CONV_W - 1, C), cc_spec,
    ]
    out_shape = [jax.ShapeDtypeStruct((B, Tt, D), F32)] + [
        jax.ShapeDtypeStruct(a.shape, F32) for a in states]
    aliases = {}
    alias_args = []
    if prev_states is not None:
        alias_args = list(prev_states)
        in_specs += [pl.BlockSpec(memory_space=pl.ANY)] * len(alias_args)
        aliases = {N_MIXER_INPUTS + k: 1 + k for k in range(len(alias_args))}
    tile_major = pltpu.VMEM((R // SUBLANES, LANE_BLOCKS * SUBLANES, LANES), F32)
    scratch = [
        pltpu.VMEM((R, D), BF16),
        pltpu.VMEM((S, RC_HIST + T, C), F32),
        pltpu.VMEM((R, C), F32),
        tile_major, tile_major, tile_major, tile_major,
        pltpu.VMEM((S * (CC_HIST + T), SUBLANES, LANES), F32),
        pltpu.VMEM((R // 2 + CC_HIST * (1 if S == 1 else S // 2), 2 * SUBLANES, LANES), BF16),
        pltpu.VMEM((S, SUBLANES, LANES), F32),
        pltpu.VMEM((CC_HIST // SUBLANES, LANE_BLOCKS * SUBLANES, LANES), F32),
    ]
    args = [x, mod, h0, rc0, cc0,
            w["norm1_g"], w["w_in"], w["rg_conv_w"], w["rg_conv_b"], w["gate_w"],
            w["rg_ba"], w["rg_bx"], w["rg_lambda"], w["w_branch_a"],
            w["cc_dw_w"], w["cc_dw_b"], w["cc_ln_g"], w["cc_ln_b"], w["w_branch_b"],
            w["b_branch_b"], w["w_out"]]
    assert len(args) == N_MIXER_INPUTS
    outs = pl.pallas_call(
        functools.partial(_mixer_kernel_with_aliases, n_alias=len(alias_args), S=S, T=T, n_chunks=n_chunks,
                          cc_time_major=cc_time_major),
        grid=(B // S, n_chunks),
        in_specs=in_specs,
        out_specs=out_specs,
        out_shape=out_shape,
        scratch_shapes=scratch,
        input_output_aliases=aliases,
        compiler_params=pltpu.CompilerParams(
            dimension_semantics=("arbitrary", "arbitrary"),
            vmem_limit_bytes=VMEM_LIMIT_BYTES),
        name=f"mixer_S{S}",
    )(*args, *alias_args)
    return outs[0], tuple(outs[1:])


def _finish(x3, g2, f, fg_ref, o_ref):
    y = x3 + (1.0 + g2) * f.reshape(x3.shape)
    if fg_ref is not None:
        y = _rmsnorm(y, fg_ref[...])
    o_ref[...] = y


def _ffn_kernel(x_ref, mod_ref, n2g_ref, wg_ref, wu_ref, wd_ref, *rest, n_split, final):
    fg_ref = rest[0] if final else None
    o_ref = rest[-1]
    x3 = x_ref[...]
    S, T, D = x3.shape
    sh2 = mod_ref[:, 3:4, :]
    sc2 = mod_ref[:, 4:5, :]
    g2 = mod_ref[:, 5:6, :]
    hn = (_rmsnorm(x3, n2g_ref[...]) * (1.0 + sc2) + sh2).reshape(S * T, D).astype(BF16)
    ff = wg_ref.shape[1] // n_split
    f = None
    for j in range(n_split):
        cs = slice(j * ff, (j + 1) * ff)
        h = (_silu(_dot(hn, wg_ref[:, cs])) * _dot(hn, wu_ref[:, cs])).astype(BF16)
        part = _dot(h, wd_ref[cs, :])
        f = part if f is None else f + part
    _finish(x3, g2, f, fg_ref, o_ref)


def _ffn(l, j, x, mod, w, final_g, *, S, T):
    B, Tt, D = x.shape
    ff = w["ffn_w_gate"].shape[-1]
    final = final_g is not None
    in_specs = [
        pl.BlockSpec((S, T, D), lambda b, t: (b, t, 0)),
        pl.BlockSpec((S, 6, D), lambda b, t: (b, 0, 0)),
        _resident((None, 1, D), lambda b, t: (l, 0, 0)),
        _resident((None, D, ff), lambda b, t: (j, 0, 0)),
        _resident((None, D, ff), lambda b, t: (j, 0, 0)),
        _resident((None, ff, D), lambda b, t: (j, 0, 0)),
    ]
    args = [x, mod, w["norm2_g"], w["ffn_w_gate"], w["ffn_w_up"], w["ffn_w_down"]]
    if final:
        in_specs.append(_resident((1, D), lambda b, t: (0, 0)))
        args.append(final_g)
    return pl.pallas_call(
        functools.partial(_ffn_kernel, n_split=2, final=final),
        grid=(B // S, Tt // T),
        in_specs=in_specs,
        out_specs=pl.BlockSpec((S, T, D), lambda b, t: (b, t, 0)),
        out_shape=jax.ShapeDtypeStruct((B, Tt, D), F32),
        compiler_params=pltpu.CompilerParams(
            dimension_semantics=("arbitrary", "arbitrary"),
            vmem_limit_bytes=VMEM_LIMIT_BYTES),
        name=f"ffn_S{S}",
    )(*args)


MOE_BLOCK = 1024
MOE_ROWS = 128
MOE_SLOTS = TOP_K * MOE_BLOCK + N_EXPERTS * MOE_ROWS


MOE_FINISH_ROWS = 256


def _adaln2(x3, mod_ref, n2g_ref, rows):
    return _rmsnorm(x3, n2g_ref[...]) * (1.0 + mod_ref[rows, 4:5, :]) + mod_ref[rows, 3:4, :]


def _route_kernel(x_ref, mod_ref, n2g_ref, wr_ref, br_ref, tri_ref, meta_ref, info_ref):
    x3 = x_ref[...]
    S, T, D = x3.shape
    R = S * T
    hn = _adaln2(x3, mod_ref, n2g_ref, slice(None)).reshape(R, D)
    hn_hi = hn.astype(BF16)
    hn_lo = (hn - hn_hi.astype(F32)).astype(BF16)
    both = _dot(hn_hi, wr_ref[...])
    logits = both[:, :LANES] + both[:, LANES:] + _dot(hn_lo, wr_ref[:, :LANES]) + br_ref[...]
    z = logits - jnp.max(logits, axis=-1, keepdims=True)
    p = jnp.exp(z)
    p = p / jnp.sum(p, axis=-1, keepdims=True)
    lane_i = lax.broadcasted_iota(jnp.int32, p.shape, 1)
    lane = lane_i.astype(F32)
    far = jnp.float32(p.shape[1])
    m1 = jnp.max(p, axis=-1, keepdims=True)
    i1 = jnp.min(jnp.where(p == m1, lane, far), axis=-1, keepdims=True)
    pick1 = lane == i1
    p2 = jnp.where(pick1, -1.0, p)
    m2 = jnp.max(p2, axis=-1, keepdims=True)
    i2 = jnp.min(jnp.where(p2 == m2, lane, far), axis=-1, keepdims=True)
    pick2 = lane == i2
    onehot = jnp.where(pick1, 1.0, 0.0) + jnp.where(pick2, 1.0, 0.0)
    rank = _dot(tri_ref[...], onehot.astype(BF16))
    count = rank[R - 1:R, :] + onehot[R - 1:R, :]
    count8 = jnp.broadcast_to(count, (SUBLANES, LANES))
    padded = jnp.floor((count8 + (MOE_ROWS - 1)) * (1.0 / MOE_ROWS)) * MOE_ROWS
    incl = padded
    lane8 = lax.broadcasted_iota(jnp.int32, (SUBLANES, LANES), 1)
    for k in (1, 2, 4):
        incl = incl + jnp.where(lane8 >= k, pltpu.roll(incl, k, 1), 0.0)
    start8 = incl - padded
    slot = start8[0:1, :] + rank
    slot1 = jnp.sum(jnp.where(pick1, slot, 0.0), axis=-1, keepdims=True)
    slot2 = jnp.sum(jnp.where(pick2, slot, 0.0), axis=-1, keepdims=True)
    den = m1 + m2
    meta_ref[...] = jnp.where(lane_i == 0, slot1, jnp.where(lane_i == 1, slot2,
                              jnp.where(lane_i == 2, m1 / den, jnp.where(lane_i == 3, m2 / den, 0.0))))
    sub8 = lax.broadcasted_iota(jnp.int32, (SUBLANES, LANES), 0)
    info_ref[...] = jnp.where(sub8 == 0, start8, jnp.where(sub8 == 1, count8, 0.0))


def _route(l, j, x, mod, w, *, S, T):
    B, Tt, D = x.shape
    R = S * T
    assert R == MOE_BLOCK
    nt = Tt // T
    nblk = (B // S) * nt
    return pl.pallas_call(
        _route_kernel,
        grid=(B // S, nt),
        in_specs=[
            pl.BlockSpec((S, T, D), lambda b, t: (b, t, 0)),
            pl.BlockSpec((S, 6, D), lambda b, t: (b, 0, 0)),
            _resident((None, 1, D), lambda b, t: (l, 0, 0)),
            _resident((None, D, 2 * LANES), lambda b, t: (j, 0, 0)),
            _resident((None, 1, LANES), lambda b, t: (j, 0, 0)),
            _resident((R, R), lambda b, t: (0, 0)),
        ],
        out_specs=[
            pl.BlockSpec((None, R, LANES), lambda b, t: (b * nt + t, 0, 0)),
            pl.BlockSpec((None, SUBLANES, LANES), lambda b, t: (b * nt + t, 0, 0)),
        ],
        out_shape=[
            jax.ShapeDtypeStruct((nblk, R, LANES), F32),
            jax.ShapeDtypeStruct((nblk, SUBLANES, LANES), F32),
        ],
        compiler_params=pltpu.CompilerParams(
            dimension_semantics=("arbitrary", "arbitrary"),
            vmem_limit_bytes=VMEM_LIMIT_BYTES),
        name=f"route_S{S}",
    )(x, mod, w["norm2_g"], w["router_w"], w["router_b"], jnp.tril(jnp.ones((R, R), BF16), -1))


def _row_chunks(S, T, rows):
    if S == 1:
        return [(slice(0, 1), slice(r0, r0 + rows), r0) for r0 in range(0, T, rows)]
    per = rows // T
    return [(slice(s0, s0 + per), slice(None), s0 * T) for s0 in range(0, S, per)]


def _moe_kernel(start_ref, count_ref, x_ref, mod_ref, n2g_ref, slot_ref, gate_ref, wg_ref, wu_ref, wd_ref,
                *rest, final):
    fg_ref = rest[0] if final else None
    o_ref, hn_tm, xs_tm, y_tm, ys, inv = rest[-6:]
    b = pl.program_id(0)
    e = pl.program_id(1)
    S, T, D = x_ref.shape
    TB = S * T
    pieces = _row_chunks(S, T, MOE_FINISH_ROWS)

    @pl.when(e == 0)
    def _prepare():
        for ss, ts, r0 in pieces:
            hn = _adaln2(x_ref[ss, ts, :], mod_ref, n2g_ref, ss)
            _rows_to_tile_major(hn_tm, hn.reshape(MOE_FINISH_ROWS, D), 0, r0 // SUBLANES)

        def clear(i, c):
            inv[i] = 0
            return c
        for ex in range(N_EXPERTS):
            cnt = count_ref[b * N_EXPERTS + ex]
            lo = start_ref[b * N_EXPERTS + ex] + cnt
            lax.fori_loop(lo, lo + lax.rem(MOE_ROWS - lax.rem(cnt, MOE_ROWS), MOE_ROWS), clear, 0)

        def fill(i, c):
            inv[slot_ref[0, TOP_K * i]] = i
            inv[slot_ref[0, TOP_K * i + 1]] = i
            return c
        lax.fori_loop(0, TB, fill, 0, unroll=SUBLANES)

    first = start_ref[b * N_EXPERTS + e]
    n_rows = count_ref[b * N_EXPERTS + e]
    n_chunks = lax.div(n_rows + (MOE_ROWS - 1), MOE_ROWS)

    last = N_EXPERTS - 1
    last_base = (start_ref[b * N_EXPERTS + last]
                 + lax.div(count_ref[b * N_EXPERTS + last] + (MOE_ROWS - 1), MOE_ROWS) * MOE_ROWS - MOE_ROWS)

    def gather(base):
        dst = xs_tm.at[lax.bitwise_and(lax.div(base, MOE_ROWS), 1)]
        for r in range(MOE_ROWS):
            t = inv[base + r]
            _store_token(dst, r // SUBLANES, r % SUBLANES,
                         _load_token(hn_tm, lax.shift_right_logical(t, 3), lax.bitwise_and(t, 7)))

    @pl.when(e == 0)
    def _first_gather():
        gather(0)

    def chunk(c, carry):
        base = first + c * MOE_ROWS
        lhs = _tile_major_to_rows(xs_tm.at[lax.bitwise_and(lax.div(base, MOE_ROWS), 1)]).astype(BF16)
        gather(jnp.minimum(base + MOE_ROWS, last_base))
        h = (_silu(_dot(lhs, wg_ref[...])) * _dot(lhs, wu_ref[...])).astype(BF16)
        _rows_to_tile_major(y_tm, _dot(h, wd_ref[...]))
        for r in range(MOE_ROWS):
            ys[base + r] = _load_token(y_tm, r // SUBLANES, r % SUBLANES)
        return carry

    lax.fori_loop(0, n_chunks, chunk, 0)

    @pl.when(e == N_EXPERTS - 1)
    def _combine():
        def comb(i8, c):
            for i in range(SUBLANES):
                t = i8 * SUBLANES + i
                f = (gate_ref[0, TOP_K * t] * ys[slot_ref[0, TOP_K * t]]
                     + gate_ref[0, TOP_K * t + 1] * ys[slot_ref[0, TOP_K * t + 1]])
                _store_token(hn_tm, i8, i, f)
            return c
        lax.fori_loop(0, TB // SUBLANES, comb, 0)
        for ss, ts, r0 in pieces:
            x3 = x_ref[ss, ts, :]
            f = _tile_major_to_rows(hn_tm, r0 // SUBLANES, MOE_FINISH_ROWS // SUBLANES)
            y = x3 + (1.0 + mod_ref[ss, 5:6, :]) * f.reshape(x3.shape)
            if final:
                y = _rmsnorm(y, fg_ref[...])
            o_ref[ss, ts, :] = y


def _moe(l, j, x, mod, w, final_g, *, S, T):
    B, Tt, D = x.shape
    ff = w["moe_w_gate"].shape[-1]
    final = final_g is not None
    TB = S * T
    nt = Tt // T
    nblk = (B // S) * nt
    meta, info = _route(l, j, x, mod, w, S=S, T=T)

    slots = meta[:, :, 0:TOP_K].astype(jnp.int32).reshape(nblk, 1, TOP_K * TB)
    gates = meta[:, :, TOP_K:2 * TOP_K].reshape(nblk, 1, TOP_K * TB)
    starts = info[:, 0, :N_EXPERTS].astype(jnp.int32).reshape(nblk * N_EXPERTS)
    counts = info[:, 1, :N_EXPERTS].astype(jnp.int32).reshape(nblk * N_EXPERTS)

    smem_row = lambda: pl.BlockSpec((None, 1, TOP_K * TB), lambda b, e, st, ct: (b, 0, 0), memory_space=pltpu.SMEM)
    in_specs = [
        pl.BlockSpec((S, T, D), lambda b, e, st, ct: (b // nt, b % nt, 0)),
        pl.BlockSpec((S, 6, D), lambda b, e, st, ct: (b // nt, 0, 0)),
        pl.BlockSpec((None, 1, D), lambda b, e, st, ct: (l, 0, 0)),
        smem_row(), smem_row(),
        pl.BlockSpec((None, None, D, ff), lambda b, e, st, ct: (j, e, 0, 0)),
        pl.BlockSpec((None, None, D, ff), lambda b, e, st, ct: (j, e, 0, 0)),
        pl.BlockSpec((None, None, ff, D), lambda b, e, st, ct: (j, e, 0, 0)),
    ]
    args = [x, mod, w["norm2_g"], slots, gates, w["moe_w_gate"], w["moe_w_up"], w["moe_w_down"]]
    if final:
        in_specs.append(pl.BlockSpec((1, D), lambda b, e, st, ct: (0, 0)))
        args.append(final_g)
    tile_major = lambda rows: pltpu.VMEM((rows // SUBLANES, LANE_BLOCKS * SUBLANES, LANES), F32)
    return pl.pallas_call(
        functools.partial(_moe_kernel, final=final),
        grid_spec=pltpu.PrefetchScalarGridSpec(
            num_scalar_prefetch=2,
            grid=(nblk, N_EXPERTS),
            in_specs=in_specs,
            out_specs=pl.BlockSpec((S, T, D), lambda b, e, st, ct: (b // nt, b % nt, 0)),
            scratch_shapes=[
                tile_major(TB),
                pltpu.VMEM((2, MOE_ROWS // SUBLANES, LANE_BLOCKS * SUBLANES, LANES), F32),
                tile_major(MOE_ROWS),
                pltpu.VMEM((MOE_SLOTS, SUBLANES, LANES), F32),
                pltpu.SMEM((MOE_SLOTS,), jnp.int32),
            ]),
        out_shape=jax.ShapeDtypeStruct((B, Tt, D), F32),
        compiler_params=pltpu.CompilerParams(
            dimension_semantics=("arbitrary", "arbitrary"),
            vmem_limit_bytes=VMEM_LIMIT_BYTES),
        name=f"moe_S{S}",
    )(starts, counts, *args)


def _block_diag_gate_weights(wa, wx):
    def bd(w):
        L, H, bs, _ = w.shape
        hg = GATE_GROUP // bs
        w5 = w.reshape(L, H // hg, hg, bs, bs)
        eye = jnp.eye(hg, dtype=w.dtype)
        m = jnp.einsum("lqhij,hk->lqhikj", w5, eye)
        return m.reshape(L, H // hg, GATE_GROUP, GATE_GROUP)
    return jnp.concatenate([bd(wa), bd(wx)], axis=-1)


def kernel(x_prompt, x_sample, c_prompt, c_sample, state_rglru_h, state_rglru_conv, state_cconv,
           ada_w, ada_b, norm1_g, norm2_g, w_in, rg_conv_w, rg_conv_b, rg_wa, rg_ba, rg_wx, rg_bx,
           rg_lambda, w_branch_a, cc_dw_w, cc_dw_b, cc_ln_g, cc_ln_b, w_branch_b, b_branch_b, w_out,
           ffn_w_gate, ffn_w_up, ffn_w_down, moe_router_w, moe_router_b, moe_w_gate, moe_w_up,
           moe_w_down, final_norm_g):
    depth = ada_w.shape[0]
    D = D_MODEL
    row = lambda a: a.reshape(a.shape[0], 1, a.shape[-1])
    n_moe = moe_router_w.shape[0]
    router_w = jnp.zeros((n_moe, D, LANES), F32).at[:, :, :N_EXPERTS].set(moe_router_w)
    router_hi = router_w.astype(BF16)
    router_lo = (router_w - router_hi.astype(F32)).astype(BF16)
    router_w = jnp.concatenate([router_hi, router_lo], axis=-1)
    router_b = jnp.full((n_moe, 1, LANES), -1e30, F32).at[:, 0, :N_EXPERTS].set(moe_router_b)
    w = {
        "norm1_g": row(norm1_g), "norm2_g": row(norm2_g),
        "w_in": w_in.astype(BF16),
        "rg_conv_w": rg_conv_w, "rg_conv_b": row(rg_conv_b),
        "gate_w": _block_diag_gate_weights(rg_wa, rg_wx).astype(BF16),
        "rg_ba": row(rg_ba), "rg_bx": row(rg_bx), "rg_lambda": row(rg_lambda),
        "w_branch_a": w_branch_a.astype(BF16),
        "cc_dw_w": jnp.tile(cc_dw_w.reshape(depth, CC_CONV_W, SUBLANES, LANES), (1, 1, 2, 1)).astype(BF16),
        "cc_dw_b": cc_dw_b.reshape(depth, SUBLANES, LANES),
        "cc_ln_g": row(cc_ln_g), "cc_ln_b": row(cc_ln_b),
        "w_branch_b": w_branch_b.astype(BF16), "b_branch_b": row(b_branch_b),
        "w_out": w_out.astype(BF16),
        "ffn_w_gate": ffn_w_gate.astype(BF16), "ffn_w_up": ffn_w_up.astype(BF16),
        "ffn_w_down": ffn_w_down.astype(BF16),
        "router_w": router_w, "router_b": router_b,
        "moe_w_gate": moe_w_gate.astype(BF16), "moe_w_up": moe_w_up.astype(BF16),
        "moe_w_down": moe_w_down.astype(BF16),
    }
    final_g = final_norm_g.reshape(1, D)

    nb_p = x_prompt.shape[0]
    nb_s = x_sample.shape[0]
    mod_all = _ada(jnp.concatenate([c_prompt, c_sample], axis=0), ada_w, ada_b)
    mod_all = mod_all.reshape(depth, nb_p + nb_s, 6, D)

    dt = x_prompt.dtype
    zeros_p = (jnp.zeros((depth, nb_p, 1, D), dt),
               jnp.zeros((depth, nb_p, RG_CONV_W - 1, D), dt),
               jnp.zeros((depth, nb_p, CC_CONV_W - 1, D), dt))
    state_s = (state_rglru_h.reshape(depth, nb_s, 1, D), state_rglru_conv, state_cconv)

    def trunk(x, mod, states, *, mix_tile, ffn_tile, moe_tile, cc_time_major=False):
        if cc_time_major:
            states = (states[0], states[1], jnp.swapaxes(states[2], 1, 2))
        new_states = None
        for l in range(depth):
            x, new_states = _mixer(l, x, mod[l], states, new_states, w, S=mix_tile[0], T=mix_tile[1],
                                   cc_time_major=cc_time_major)
            fg = final_g if l == depth - 1 else None
            if l % 2 == 0:
                x = _ffn(l, l // 2, x, mod[l], w, fg, S=ffn_tile[0], T=ffn_tile[1])
            else:
                x = _moe(l, l // 2, x, mod[l], w, fg, S=moe_tile[0], T=moe_tile[1])
        h, rc, cc = new_states
        if cc_time_major:
            cc = jnp.swapaxes(cc, 1, 2)
        return x, h[:, :, 0, :], rc, cc

    y_p, h_p, rc_p, cc_p = trunk(x_prompt, mod_all[:, :nb_p], zeros_p,
                                 mix_tile=(1, 256), ffn_tile=(1, 512), moe_tile=(1, MOE_BLOCK))
    y_s, h_s, rc_s, cc_s = trunk(x_sample, mod_all[:, nb_p:], state_s,
                                 mix_tile=(16, 8), ffn_tile=(64, 8), moe_tile=(MOE_BLOCK // 8, 8),
                                 cc_time_major=True)
    return (y_p, y_s, h_p, rc_p, cc_p, h_s, rc_s, cc_s)
```

```python
import functools

import jax
import jax.numpy as jnp
from jax import lax
from jax.experimental import pallas as pl
from jax.experimental.pallas import tpu as pltpu

D_MODEL = 1024
N_RNN_HEADS = 16
RG_CONV_W = 4
RG_C = 8.0
CC_CONV_W = 31
N_EXPERTS = 8
TOP_K = 2
EPS = 1e-6

LANES = 128
SUBLANES = 8
VMEM_LIMIT_BYTES = 56 * 1024 * 1024

GATE_GROUP = 256
RC_HIST = SUBLANES
CC_HIST = 32
CONV_BLOCK = 4

BF16 = jnp.bfloat16
F32 = jnp.float32


def _sigmoid(x):
    return jax.nn.sigmoid(x)


def _silu(x):
    return x * _sigmoid(x)


def _gelu_tanh(x):
    c = 0.7978845608028654
    return 0.5 * x * (1.0 + jnp.tanh(c * (x + 0.044715 * (x * x * x))))


def _softplus(x):
    return jnp.maximum(x, 0.0) + jnp.log1p(jnp.exp(-jnp.abs(x)))


def _rmsnorm(x, g):
    ms = jnp.mean(x * x, axis=-1, keepdims=True)
    return x * lax.rsqrt(ms + EPS) * g


def _dot(a, b):
    return jnp.dot(a, b, preferred_element_type=F32)


def _resident(block_shape, index_map):
    return pl.BlockSpec(block_shape, index_map, pipeline_mode=pl.Buffered(1))


def _ada_kernel(c_ref, w_ref, b_ref, o_ref):
    s = _silu(c_ref[...]).astype(BF16)
    o_ref[...] = _dot(s, w_ref[...].astype(BF16)) + b_ref[...]


def _ada(c_all, ada_w, ada_b):
    depth, d, n = ada_w.shape
    nb = c_all.shape[0]
    tn = 1536
    return pl.pallas_call(
        _ada_kernel,
        grid=(depth, n // tn),
        in_specs=[
            pl.BlockSpec((nb, d), lambda l, j: (0, 0)),
            pl.BlockSpec((None, d, tn), lambda l, j: (l, 0, j)),
            pl.BlockSpec((None, 1, tn), lambda l, j: (l, 0, j)),
        ],
        out_specs=pl.BlockSpec((None, nb, tn), lambda l, j: (l, 0, j)),
        out_shape=jax.ShapeDtypeStruct((depth, nb, n), F32),
        compiler_params=pltpu.CompilerParams(
            dimension_semantics=("arbitrary", "arbitrary"),
            vmem_limit_bytes=VMEM_LIMIT_BYTES),
        name="ada_mod",
    )(c_all, ada_w, ada_b.reshape(depth, 1, n))


LANE_BLOCKS = D_MODEL // LANES


def _rows_to_tile_major(tm_ref, val, lane_block0=0, row_block0=0):
    nb = val.shape[0] // SUBLANES
    for j in range(val.shape[1] // LANES):
        lo = (lane_block0 + j) * SUBLANES
        tm_ref[row_block0:row_block0 + nb, lo:lo + SUBLANES, :] = (
            val[:, j * LANES:(j + 1) * LANES].reshape(nb, SUBLANES, LANES))


def _tile_major_to_rows(tm_ref, row_block0=0, n_blocks=None):
    nb = tm_ref.shape[0] if n_blocks is None else n_blocks
    return jnp.concatenate(
        [tm_ref[row_block0:row_block0 + nb, j * SUBLANES:(j + 1) * SUBLANES, :].reshape(nb * SUBLANES, LANES)
         for j in range(LANE_BLOCKS)], axis=1)


def _load_token(tm_ref, blk, sub):
    return tm_ref[blk, pl.ds(sub, SUBLANES, stride=SUBLANES), :]


def _store_token(tm_ref, blk, sub, val):
    tm_ref[blk, pl.ds(sub, SUBLANES, stride=SUBLANES), :] = val


def _mixer_kernel(x_ref, mod_ref, h0_ref, rc0_ref, cc0_ref,
                  n1g_ref, win_ref, rgw_ref, rgb_ref, gw_ref, ba_ref, bx_ref,
                  lam_ref, wa_ref, ccw_ref, ccb_ref, lng_ref, lnb_ref, wb_ref,
                  bb_ref, wo_ref,
                  xo_ref, ho_ref, rco_ref, cco_ref,
                  hn_s, xbuf, xc_s, a_tm, b_tm, u_tm, y_tm, vt, vtb, hcar, st_tm,
                  *, S, T, n_chunks, cc_time_major, out_layer):
    C = D_MODEL
    stacked_out = (ho_ref, rco_ref, cco_ref)
    if out_layer is not None:
        ho_ref, rco_ref, cco_ref = (r.at[out_layer] for r in stacked_out)
    R = S * T
    HT = CC_HIST + T
    t_idx = pl.program_id(1)

    @pl.when(t_idx == 0)
    def _load_state():
        xbuf[:, RC_HIST - (RG_CONV_W - 1):RC_HIST, :] = rc0_ref[...]
        hist0 = CC_HIST - (CC_CONV_W - 1)
        if cc_time_major:
            for p in range(CC_CONV_W - 1):
                _rows_to_tile_major(st_tm, cc0_ref[p])
                for s in range(S):
                    vt[s * HT + hist0 + p] = _load_token(st_tm, *divmod(s, SUBLANES))
        else:
            pad_rows = jnp.zeros((hist0, C), F32)
            for s in range(S):
                _rows_to_tile_major(st_tm, jnp.concatenate([cc0_ref[s], pad_rows], axis=0))
                for p in range(CC_CONV_W - 1):
                    vt[s * HT + hist0 + p] = _load_token(st_tm, *divmod(p, SUBLANES))
        for s in range(S):
            _rows_to_tile_major(st_tm, jnp.broadcast_to(h0_ref[s], (SUBLANES, C)))
            hcar[s] = _load_token(st_tm, 0, 0)

    x3 = x_ref[...]
    sh1 = mod_ref[:, 0:1, :]
    sc1 = mod_ref[:, 1:2, :]
    g1 = mod_ref[:, 2:3, :]
    hn = _rmsnorm(x3, n1g_ref[...]) * (1.0 + sc1) + sh1
    hn_s[...] = hn.reshape(R, C).astype(BF16)

    def proj(j):
        return _dot(hn_s[...], win_ref[:, j * C:(j + 1) * C])

    xbuf[:, RC_HIST:RC_HIST + T, :] = proj(0).reshape(S, T, C)
    xc3 = rgb_ref[...] + rgw_ref[0:1, :] * xbuf[:, RC_HIST - 3:RC_HIST - 3 + T, :]
    for k in range(1, RG_CONV_W):
        lo = RC_HIST - 3 + k
        xc3 = xc3 + rgw_ref[k:k + 1, :] * xbuf[:, lo:lo + T, :]
    xc_s[...] = xc3.reshape(R, C)

    for g in range(C // GATE_GROUP):
        cs = slice(g * GATE_GROUP, (g + 1) * GATE_GROUP)
        xg = xc_s[:, cs]
        pre = _dot(xg.astype(BF16), gw_ref[g])
        r = _sigmoid(pre[:, :GATE_GROUP] + ba_ref[:, cs])
        i = _sigmoid(pre[:, GATE_GROUP:] + bx_ref[:, cs])
        log_a = (-RG_C * _softplus(-lam_ref[:, cs])) * r
        a = jnp.exp(log_a)
        mult = jnp.sqrt(jnp.tanh(-log_a) * (a * a + 1.0))
        _rows_to_tile_major(a_tm, a, g * (GATE_GROUP // LANES))
        _rows_to_tile_major(b_tm, mult * (i * xg), g * (GATE_GROUP // LANES))

    for s in range(S):
        h = hcar[s]
        for t in range(T):
            blk, sub = divmod(s * T + t, SUBLANES)
            h = _load_token(a_tm, blk, sub) * h + _load_token(b_tm, blk, sub)
            _store_token(b_tm, blk, sub, h)
        hcar[s] = h

    ya_in = (_gelu_tanh(proj(1)) * _tile_major_to_rows(b_tm)).astype(BF16)
    ya = _dot(ya_in, wa_ref[...])

    _rows_to_tile_major(u_tm, proj(2) * _sigmoid(proj(3)))
    for s in range(S):
        for t in range(T):
            blk, sub = divmod(s * T + t, SUBLANES)
            vt[s * HT + CC_HIST + t] = _load_token(u_tm, blk, sub)

    n_streams, Lp = (1, T // 2) if S == 1 else (S // 2, T)
    hi_off = T // 2 if S == 1 else (S // 2) * HT
    HP = CC_HIST + Lp
    first_tap = CC_HIST - (CC_CONV_W - 1)
    for q in range(n_streams):
        for p in range(first_tap, HP):
            lo = q * HT + p
            vtb[q * HP + p] = jnp.concatenate([vt[lo], vt[lo + hi_off]], axis=0).astype(BF16)

    taps = [ccw_ref[k] for k in range(CC_CONV_W)]
    bias = ccb_ref[...]
    blocks_per_stream = Lp // SUBLANES

    def conv_rows(bi, carry):
        q = 0 if n_streams == 1 else bi // blocks_per_stream
        base = bi * SUBLANES + q * CC_HIST + first_tap
        for i0 in range(0, SUBLANES, CONV_BLOCK):
            acc = [None] * CONV_BLOCK
            for j in range(CONV_BLOCK + CC_CONV_W - 1):
                xj = vtb[base + i0 + j].astype(F32)
                for i in range(CONV_BLOCK):
                    k = j - i
                    if 0 <= k < CC_CONV_W:
                        prod = xj * taps[k].astype(F32)
                        acc[i] = prod if acc[i] is None else acc[i] + prod
            for i in range(CONV_BLOCK):
                _store_token(y_tm, bi, i0 + i, acc[i][:SUBLANES] + bias)
                _store_token(y_tm, bi + R // (2 * SUBLANES), i0 + i, acc[i][SUBLANES:] + bias)
        return carry

    lax.fori_loop(0, R // (2 * SUBLANES), conv_rows, 0)

    v = _tile_major_to_rows(y_tm)
    mu = jnp.mean(v, axis=-1, keepdims=True)
    vc = v - mu
    vn = vc * lax.rsqrt(jnp.mean(vc * vc, axis=-1, keepdims=True) + EPS)
    vn = _silu(vn * lng_ref[...] + lnb_ref[...])
    yb = _dot(vn.astype(BF16), wb_ref[...]) + bb_ref[...]

    m = _sigmoid(proj(4)) * ya + _sigmoid(proj(5)) * yb
    o = _dot(m.astype(BF16), wo_ref[...])
    xo_ref[...] = x3 + (1.0 + g1) * o.reshape(S, T, C)

    @pl.when(t_idx == n_chunks - 1)
    def _write_state():
        if out_layer is not None:
            for r in stacked_out:
                for li in range(r.shape[0]):
                    if li != out_layer:
                        r[li] = jnp.zeros(r.shape[1:], F32)
        rco_ref[...] = xbuf[:, RC_HIST + T - (RG_CONV_W - 1):RC_HIST + T, :]
        for s in range(S):
            _store_token(st_tm, 0, 0, hcar[s])
            ho_ref[s] = _tile_major_to_rows(st_tm, 0, 1)[0:1, :]
        if cc_time_major:
            for p in range(CC_CONV_W - 1):
                for s in range(S):
                    _store_token(st_tm, *divmod(s, SUBLANES), vt[s * HT + HT - (CC_CONV_W - 1) + p])
                cco_ref[p] = _tile_major_to_rows(st_tm, 0, S // SUBLANES)
        else:
            for s in range(S):
                for p in range(CC_CONV_W - 1):
                    _store_token(st_tm, *divmod(p, SUBLANES), vt[s * HT + HT - (CC_CONV_W - 1) + p])
                cco_ref[s] = _tile_major_to_rows(st_tm)[0:CC_CONV_W - 1, :]

    if n_chunks > 1:
        xbuf[:, 0:RC_HIST, :] = xbuf[:, T:T + RC_HIST, :]
        for s in range(S):
            vt[s * HT:s * HT + CC_HIST] = vt[s * HT + T:s * HT + T + CC_HIST]


N_MIXER_INPUTS = 21


def _mixer_kernel_with_aliases(*refs, n_alias, **static):
    _mixer_kernel(*refs[:N_MIXER_INPUTS], *refs[N_MIXER_INPUTS + n_alias:], **static)


def _mixer(l, x, mod, states, prev_states, w, *, S, T, cc_time_major=False):
    h0, rc0, cc0 = states
    assert cc0.shape[1:3] == ((CC_CONV_W - 1, x.shape[0]) if cc_time_major else (x.shape[0], CC_CONV_W - 1))
    B, Tt, D = x.shape
    C = D
    n_chunks = Tt // T
    assert B % S == 0 and Tt % T == 0 and T % SUBLANES == 0
    assert S == 1 or n_chunks == 1
    assert n_chunks == 1 or T >= CC_HIST
    R = S * T

    def seq(*tail):
        return pl.BlockSpec((S,) + tail, lambda b, t: (b,) + (0,) * len(tail))

    def state(*tail):
        return pl.BlockSpec((None, S) + tail, lambda b, t: (l, b) + (0,) * len(tail))

    n_layers = None if prev_states is not None else h0.shape[0]

    def out_state(*tail):
        return pl.BlockSpec((n_layers, S) + tail, lambda b, t: (0 if n_layers else l, b) + (0,) * len(tail))

    if cc_time_major:
        assert S % SUBLANES == 0 and S // SUBLANES <= CC_HIST // SUBLANES
        cc_spec = pl.BlockSpec((None, CC_CONV_W - 1, S, C), lambda b, t: (l, 0, b, 0))
        cc_out_spec = pl.BlockSpec((n_layers, CC_CONV_W - 1, S, C), lambda b, t: (0 if n_layers else l, 0, b, 0))
    else:
        cc_spec = state(CC_CONV_W - 1, C)
        cc_out_spec = out_state(CC_CONV_W - 1, C)

    def layer(*tail):
        return _resident((None,) + tail, lambda b, t: (l,) + (0,) * len(tail))

    in_specs = [
        pl.BlockSpec((S, T, D), lambda b, t: (b, t, 0)),
        seq(6, D),
        state(1, C), state(RG_CONV_W - 1, C), cc_spec,
        layer(1, D),
        layer(D, 6 * C),
        layer(RG_CONV_W, C), layer(1, C),
        layer(C // GATE_GROUP, GATE_GROUP, 2 * GATE_GROUP),
        layer(1, C), layer(1, C), layer(1, C),
        layer(C, D),
        layer(CC_CONV_W, 2 * SUBLANES, LANES), layer(SUBLANES, LANES),
        layer(1, C), layer(1, C),
        layer(C, D), layer(1, D),
        layer(D, D),
    ]
    out_specs = [
        pl.BlockSpec((S, T, D), lambda b, t: (b, t, 0)),
        out_state(1, C), out_state(RG_CONV_W - 1, C), cc_out_spec,
    ]
    out_shape = [jax.ShapeDtypeStruct((B, Tt, D), F32)] + [
        jax.ShapeDtypeStruct(a.shape, F32) for a in states]
    aliases = {}
    alias_args = []
    if prev_states is not None:
        alias_args = list(prev_states)
        in_specs += [pl.BlockSpec(memory_space=pl.ANY)] * len(alias_args)
        aliases = {N_MIXER_INPUTS + k: 1 + k for k in range(len(alias_args))}
    tile_major = pltpu.VMEM((R // SUBLANES, LANE_BLOCKS * SUBLANES, LANES), F32)
    scratch = [
        pltpu.VMEM((R, D), BF16),
        pltpu.VMEM((S, RC_HIST + T, C), F32),
        pltpu.VMEM((R, C), F32),
        tile_major, tile_major, tile_major, tile_major,
        pltpu.VMEM((S * (CC_HIST + T), SUBLANES, LANES), F32),
        pltpu.VMEM((R // 2 + CC_HIST * (1 if S == 1 else S // 2), 2 * SUBLANES, LANES), BF16),
        pltpu.VMEM((S, SUBLANES, LANES), F32),
        pltpu.VMEM((CC_HIST // SUBLANES, LANE_BLOCKS * SUBLANES, LANES), F32),
    ]
    args = [x, mod, h0, rc0, cc0,
            w["norm1_g"], w["w_in"], w["rg_conv_w"], w["rg_conv_b"], w["gate_w"],
            w["rg_ba"], w["rg_bx"], w["rg_lambda"], w["w_branch_a"],
            w["cc_dw_w"], w["cc_dw_b"], w["cc_ln_g"], w["cc_ln_b"], w["w_branch_b"],
            w["b_branch_b"], w["w_out"]]
    assert len(args) == N_MIXER_INPUTS
    outs = pl.pallas_call(
        functools.partial(_mixer_kernel_with_aliases, n_alias=len(alias_args), S=S, T=T, n_chunks=n_chunks,
                          cc_time_major=cc_time_major, out_layer=l if prev_states is None else None),
        grid=(B // S, n_chunks),
        in_specs=in_specs,
        out_specs=out_specs,
        out_shape=out_shape,
        scratch_shapes=scratch,
        input_output_aliases=aliases,
        compiler_params=pltpu.CompilerParams(
            dimension_semantics=("arbitrary", "arbitrary"),
            vmem_limit_bytes=VMEM_LIMIT_BYTES),
        name=f"mixer_S{S}",
    )(*args, *alias_args)
    return outs[0], tuple(outs[1:])


def _finish(x3, g2, f, fg_ref, o_ref):
    y = x3 + (1.0 + g2) * f.reshape(x3.shape)
    if fg_ref is not None:
        y = _rmsnorm(y, fg_ref[...])
    o_ref[...] = y


def _ffn_kernel(x_ref, mod_ref, n2g_ref, wg_ref, wu_ref, wd_ref, *rest, n_split, final):
    fg_ref = rest[0] if final else None
    o_ref = rest[-1]
    x3 = x_ref[...]
    S, T, D = x3.shape
    sh2 = mod_ref[:, 3:4, :]
    sc2 = mod_ref[:, 4:5, :]
    g2 = mod_ref[:, 5:6, :]
    hn = (_rmsnorm(x3, n2g_ref[...]) * (1.0 + sc2) + sh2).reshape(S * T, D).astype(BF16)
    ff = wg_ref.shape[1] // n_split
    f = None
    for j in range(n_split):
        cs = slice(j * ff, (j + 1) * ff)
        h = (_silu(_dot(hn, wg_ref[:, cs])) * _dot(hn, wu_ref[:, cs])).astype(BF16)
        part = _dot(h, wd_ref[cs, :])
        f = part if f is None else f + part
    _finish(x3, g2, f, fg_ref, o_ref)


def _ffn(l, j, x, mod, w, final_g, *, S, T):
    B, Tt, D = x.shape
    ff = w["ffn_w_gate"].shape[-1]
    final = final_g is not None
    in_specs = [
        pl.BlockSpec((S, T, D), lambda b, t: (b, t, 0)),
        pl.BlockSpec((S, 6, D), lambda b, t: (b, 0, 0)),
        _resident((None, 1, D), lambda b, t: (l, 0, 0)),
        _resident((None, D, ff), lambda b, t: (j, 0, 0)),
        _resident((None, D, ff), lambda b, t: (j, 0, 0)),
        _resident((None, ff, D), lambda b, t: (j, 0, 0)),
    ]
    args = [x, mod, w["norm2_g"], w["ffn_w_gate"], w["ffn_w_up"], w["ffn_w_down"]]
    if final:
        in_specs.append(_resident((1, D), lambda b, t: (0, 0)))
        args.append(final_g)
    return pl.pallas_call(
        functools.partial(_ffn_kernel, n_split=2, final=final),
        grid=(B // S, Tt // T),
        in_specs=in_specs,
        out_specs=pl.BlockSpec((S, T, D), lambda b, t: (b, t, 0)),
        out_shape=jax.ShapeDtypeStruct((B, Tt, D), F32),
        compiler_params=pltpu.CompilerParams(
            dimension_semantics=("arbitrary", "arbitrary"),
            vmem_limit_bytes=VMEM_LIMIT_BYTES),
        name=f"ffn_S{S}",
    )(*args)


MOE_BLOCK = 1024
MOE_ROWS = 128
MOE_SLOTS = TOP_K * MOE_BLOCK + N_EXPERTS * MOE_ROWS


MOE_FINISH_ROWS = 256


def _adaln2(x3, mod_ref, n2g_ref, rows):
    return _rmsnorm(x3, n2g_ref[...]) * (1.0 + mod_ref[rows, 4:5, :]) + mod_ref[rows, 3:4, :]


def _route_kernel(x_ref, mod_ref, n2g_ref, wr_ref, br_ref, tri_ref, meta_ref, info_ref):
    x3 = x_ref[...]
    S, T, D = x3.shape
    R = S * T
    hn = _adaln2(x3, mod_ref, n2g_ref, slice(None)).reshape(R, D)
    hn_hi = hn.astype(BF16)
    hn_lo = (hn - hn_hi.astype(F32)).astype(BF16)
    both = _dot(hn_hi, wr_ref[...])
    logits = both[:, :LANES] + both[:, LANES:] + _dot(hn_lo, wr_ref[:, :LANES]) + br_ref[...]
    z = logits - jnp.max(logits, axis=-1, keepdims=True)
    p = jnp.exp(z)
    p = p / jnp.sum(p, axis=-1, keepdims=True)
    lane_i = lax.broadcasted_iota(jnp.int32, p.shape, 1)
    lane = lane_i.astype(F32)
    far = jnp.float32(p.shape[1])
    m1 = jnp.max(p, axis=-1, keepdims=True)
    i1 = jnp.min(jnp.where(p == m1, lane, far), axis=-1, keepdims=True)
    pick1 = lane == i1
    p2 = jnp.where(pick1, -1.0, p)
    m2 = jnp.max(p2, axis=-1, keepdims=True)
    i2 = jnp.min(jnp.where(p2 == m2, lane, far), axis=-1, keepdims=True)
    pick2 = lane == i2
    onehot = jnp.where(pick1, 1.0, 0.0) + jnp.where(pick2, 1.0, 0.0)
    rank = _dot(tri_ref[...], onehot.astype(BF16))
    count = rank[R - 1:R, :] + onehot[R - 1:R, :]
    count8 = jnp.broadcast_to(count, (SUBLANES, LANES))
    padded = jnp.floor((count8 + (MOE_ROWS - 1)) * (1.0 / MOE_ROWS)) * MOE_ROWS
    incl = padded
    lane8 = lax.broadcasted_iota(jnp.int32, (SUBLANES, LANES), 1)
    for k in (1, 2, 4):
        incl = incl + jnp.where(lane8 >= k, pltpu.roll(incl, k, 1), 0.0)
    start8 = incl - padded
    slot = start8[0:1, :] + rank
    slot1 = jnp.sum(jnp.where(pick1, slot, 0.0), axis=-1, keepdims=True)
    slot2 = jnp.sum(jnp.where(pick2, slot, 0.0), axis=-1, keepdims=True)
    den = m1 + m2
    meta_ref[...] = jnp.where(lane_i == 0, slot1, jnp.where(lane_i == 1, slot2,
                              jnp.where(lane_i == 2, m1 / den, jnp.where(lane_i == 3, m2 / den, 0.0))))
    sub8 = lax.broadcasted_iota(jnp.int32, (SUBLANES, LANES), 0)
    info_ref[...] = jnp.where(sub8 == 0, start8, jnp.where(sub8 == 1, count8, 0.0))


def _route(l, j, x, mod, w, *, S, T):
    B, Tt, D = x.shape
    R = S * T
    assert R == MOE_BLOCK
    nt = Tt // T
    nblk = (B // S) * nt
    return pl.pallas_call(
        _route_kernel,
        grid=(B // S, nt),
        in_specs=[
            pl.BlockSpec((S, T, D), lambda b, t: (b, t, 0)),
            pl.BlockSpec((S, 6, D), lambda b, t: (b, 0, 0)),
            _resident((None, 1, D), lambda b, t: (l, 0, 0)),
            _resident((None, D, 2 * LANES), lambda b, t: (j, 0, 0)),
            _resident((None, 1, LANES), lambda b, t: (j, 0, 0)),
            _resident((R, R), lambda b, t: (0, 0)),
        ],
        out_specs=[
            pl.BlockSpec((None, R, LANES), lambda b, t: (b * nt + t, 0, 0)),
            pl.BlockSpec((None, SUBLANES, LANES), lambda b, t: (b * nt + t, 0, 0)),
        ],
        out_shape=[
            jax.ShapeDtypeStruct((nblk, R, LANES), F32),
            jax.ShapeDtypeStruct((nblk, SUBLANES, LANES), F32),
        ],
        compiler_params=pltpu.CompilerParams(
            dimension_semantics=("arbitrary", "arbitrary"),
            vmem_limit_bytes=VMEM_LIMIT_BYTES),
        name=f"route_S{S}",
    )(x, mod, w["norm2_g"], w["router_w"], w["router_b"], jnp.tril(jnp.ones((R, R), BF16), -1))


def _row_chunks(S, T, rows):
    if S == 1:
        return [(slice(0, 1), slice(r0, r0 + rows), r0) for r0 in range(0, T, rows)]
    per = rows // T
    return [(slice(s0, s0 + per), slice(None), s0 * T) for s0 in range(0, S, per)]


def _moe_kernel(start_ref, count_ref, x_ref, mod_ref, n2g_ref, slot_ref, gate_ref, wg_ref, wu_ref, wd_ref,
                *rest, final):
    fg_ref = rest[0] if final else None
    o_ref, hn_tm, xs_tm, y_tm, ys, inv = rest[-6:]
    b = pl.program_id(0)
    e = pl.program_id(1)
    S, T, D = x_ref.shape
    TB = S * T
    pieces = _row_chunks(S, T, MOE_FINISH_ROWS)

    @pl.when(e == 0)
    def _prepare():
        for ss, ts, r0 in pieces:
            hn = _adaln2(x_ref[ss, ts, :], mod_ref, n2g_ref, ss)
            _rows_to_tile_major(hn_tm, hn.reshape(MOE_FINISH_ROWS, D), 0, r0 // SUBLANES)

        def clear(i, c):
            inv[i] = 0
            return c
        for ex in range(N_EXPERTS):
            cnt = count_ref[b * N_EXPERTS + ex]
            lo = start_ref[b * N_EXPERTS + ex] + cnt
            lax.fori_loop(lo, lo + lax.rem(MOE_ROWS - lax.rem(cnt, MOE_ROWS), MOE_ROWS), clear, 0)

        def fill(i, c):
            inv[slot_ref[0, TOP_K * i]] = i
            inv[slot_ref[0, TOP_K * i + 1]] = i
            return c
        lax.fori_loop(0, TB, fill, 0, unroll=SUBLANES)

    first = start_ref[b * N_EXPERTS + e]
    n_rows = count_ref[b * N_EXPERTS + e]
    n_chunks = lax.div(n_rows + (MOE_ROWS - 1), MOE_ROWS)

    last = N_EXPERTS - 1
    last_base = (start_ref[b * N_EXPERTS + last]
                 + lax.div(count_ref[b * N_EXPERTS + last] + (MOE_ROWS - 1), MOE_ROWS) * MOE_ROWS - MOE_ROWS)

    def gather(base):
        dst = xs_tm.at[lax.bitwise_and(lax.div(base, MOE_ROWS), 1)]
        for r in range(MOE_ROWS):
            t = inv[base + r]
            _store_token(dst, r // SUBLANES, r % SUBLANES,
                         _load_token(hn_tm, lax.shift_right_logical(t, 3), lax.bitwise_and(t, 7)))

    @pl.when(e == 0)
    def _first_gather():
        gather(0)

    def chunk(c, carry):
        base = first + c * MOE_ROWS
        lhs = _tile_major_to_rows(xs_tm.at[lax.bitwise_and(lax.div(base, MOE_ROWS), 1)]).astype(BF16)
        gather(jnp.minimum(base + MOE_ROWS, last_base))
        h = (_silu(_dot(lhs, wg_ref[...])) * _dot(lhs, wu_ref[...])).astype(BF16)
        _rows_to_tile_major(y_tm, _dot(h, wd_ref[...]))
        for r in range(MOE_ROWS):
            ys[base + r] = _load_token(y_tm, r // SUBLANES, r % SUBLANES)
        return carry

    lax.fori_loop(0, n_chunks, chunk, 0)

    @pl.when(e == N_EXPERTS - 1)
    def _combine():
        def comb(i8, c):
            for i in range(SUBLANES):
                t = i8 * SUBLANES + i
                f = (gate_ref[0, TOP_K * t] * ys[slot_ref[0, TOP_K * t]]
                     + gate_ref[0, TOP_K * t + 1] * ys[slot_ref[0, TOP_K * t + 1]])
                _store_token(hn_tm, i8, i, f)
            return c
        lax.fori_loop(0, TB // SUBLANES, comb, 0)
        for ss, ts, r0 in pieces:
            x3 = x_ref[ss, ts, :]
            f = _tile_major_to_rows(hn_tm, r0 // SUBLANES, MOE_FINISH_ROWS // SUBLANES)
            y = x3 + (1.0 + mod_ref[ss, 5:6, :]) * f.reshape(x3.shape)
            if final:
                y = _rmsnorm(y, fg_ref[...])
            o_ref[ss, ts, :] = y


def _moe(l, j, x, mod, w, final_g, *, S, T):
    B, Tt, D = x.shape
    ff = w["moe_w_gate"].shape[-1]
    final = final_g is not None
    TB = S * T
    nt = Tt // T
    nblk = (B // S) * nt
    meta, info = _route(l, j, x, mod, w, S=S, T=T)

    slots = meta[:, :, 0:TOP_K].astype(jnp.int32).reshape(nblk, 1, TOP_K * TB)
    gates = meta[:, :, TOP_K:2 * TOP_K].reshape(nblk, 1, TOP_K * TB)
    starts = info[:, 0, :N_EXPERTS].astype(jnp.int32).reshape(nblk * N_EXPERTS)
    counts = info[:, 1, :N_EXPERTS].astype(jnp.int32).reshape(nblk * N_EXPERTS)

    smem_row = lambda: pl.BlockSpec((None, 1, TOP_K * TB), lambda b, e, st, ct: (b, 0, 0), memory_space=pltpu.SMEM)
    in_specs = [
        pl.BlockSpec((S, T, D), lambda b, e, st, ct: (b // nt, b % nt, 0)),
        pl.BlockSpec((S, 6, D), lambda b, e, st, ct: (b // nt, 0, 0)),
        pl.BlockSpec((None, 1, D), lambda b, e, st, ct: (l, 0, 0)),
        smem_row(), smem_row(),
        pl.BlockSpec((None, None, D, ff), lambda b, e, st, ct: (j, e, 0, 0)),
        pl.BlockSpec((None, None, D, ff), lambda b, e, st, ct: (j, e, 0, 0)),
        pl.BlockSpec((None, None, ff, D), lambda b, e, st, ct: (j, e, 0, 0)),
    ]
    args = [x, mod, w["norm2_g"], slots, gates, w["moe_w_gate"], w["moe_w_up"], w["moe_w_down"]]
    if final:
        in_specs.append(pl.BlockSpec((1, D), lambda b, e, st, ct: (0, 0)))
        args.append(final_g)
    tile_major = lambda rows: pltpu.VMEM((rows // SUBLANES, LANE_BLOCKS * SUBLANES, LANES), F32)
    return pl.pallas_call(
        functools.partial(_moe_kernel, final=final),
        grid_spec=pltpu.PrefetchScalarGridSpec(
            num_scalar_prefetch=2,
            grid=(nblk, N_EXPERTS),
            in_specs=in_specs,
            out_specs=pl.BlockSpec((S, T, D), lambda b, e, st, ct: (b // nt, b % nt, 0)),
            scratch_shapes=[
                tile_major(TB),
                pltpu.VMEM((2, MOE_ROWS // SUBLANES, LANE_BLOCKS * SUBLANES, LANES), F32),
                tile_major(MOE_ROWS),
                pltpu.VMEM((MOE_SLOTS, SUBLANES, LANES), F32),
                pltpu.SMEM((MOE_SLOTS,), jnp.int32),
            ]),
        out_shape=jax.ShapeDtypeStruct((B, Tt, D), F32),
        compiler_params=pltpu.CompilerParams(
            dimension_semantics=("arbitrary", "arbitrary"),
            vmem_limit_bytes=VMEM_LIMIT_BYTES),
        name=f"moe_S{S}",
    )(starts, counts, *args)


def _block_diag_gate_weights(wa, wx):
    def bd(w):
        L, H, bs, _ = w.shape
        hg = GATE_GROUP // bs
        w5 = w.reshape(L, H // hg, hg, bs, bs)
        eye = jnp.eye(hg, dtype=w.dtype)
        m = jnp.einsum("lqhij,hk->lqhikj", w5, eye)
        return m.reshape(L, H // hg, GATE_GROUP, GATE_GROUP)
    return jnp.concatenate([bd(wa), bd(wx)], axis=-1)


def kernel(x_prompt, x_sample, c_prompt, c_sample, state_rglru_h, state_rglru_conv, state_cconv,
           ada_w, ada_b, norm1_g, norm2_g, w_in, rg_conv_w, rg_conv_b, rg_wa, rg_ba, rg_wx, rg_bx,
           rg_lambda, w_branch_a, cc_dw_w, cc_dw_b, cc_ln_g, cc_ln_b, w_branch_b, b_branch_b, w_out,
           ffn_w_gate, ffn_w_up, ffn_w_down, moe_router_w, moe_router_b, moe_w_gate, moe_w_up,
           moe_w_down, final_norm_g):
    depth = ada_w.shape[0]
    D = D_MODEL
    row = lambda a: a.reshape(a.shape[0], 1, a.shape[-1])
    n_moe = moe_router_w.shape[0]
    router_w = jnp.zeros((n_moe, D, LANES), F32).at[:, :, :N_EXPERTS].set(moe_router_w)
    router_hi = router_w.astype(BF16)
    router_lo = (router_w - router_hi.astype(F32)).astype(BF16)
    router_w = jnp.concatenate([router_hi, router_lo], axis=-1)
    router_b = jnp.full((n_moe, 1, LANES), -1e30, F32).at[:, 0, :N_EXPERTS].set(moe_router_b)
    w = {
        "norm1_g": row(norm1_g), "norm2_g": row(norm2_g),
        "w_in": w_in.astype(BF16),
        "rg_conv_w": rg_conv_w, "rg_conv_b": row(rg_conv_b),
        "gate_w": _block_diag_gate_weights(rg_wa, rg_wx).astype(BF16),
        "rg_ba": row(rg_ba), "rg_bx": row(rg_bx), "rg_lambda": row(rg_lambda),
        "w_branch_a": w_branch_a.astype(BF16),
        "cc_dw_w": jnp.tile(cc_dw_w.reshape(depth, CC_CONV_W, SUBLANES, LANES), (1, 1, 2, 1)).astype(BF16),
        "cc_dw_b": cc_dw_b.reshape(depth, SUBLANES, LANES),
        "cc_ln_g": row(cc_ln_g), "cc_ln_b": row(cc_ln_b),
        "w_branch_b": w_branch_b.astype(BF16), "b_branch_b": row(b_branch_b),
        "w_out": w_out.astype(BF16),
        "ffn_w_gate": ffn_w_gate.astype(BF16), "ffn_w_up": ffn_w_up.astype(BF16),
        "ffn_w_down": ffn_w_down.astype(BF16),
        "router_w": router_w, "router_b": router_b,
        "moe_w_gate": moe_w_gate.astype(BF16), "moe_w_up": moe_w_up.astype(BF16),
        "moe_w_down": moe_w_down.astype(BF16),
    }
    final_g = final_norm_g.reshape(1, D)

    nb_p = x_prompt.shape[0]
    nb_s = x_sample.shape[0]
    mod_all = _ada(jnp.concatenate([c_prompt, c_sample], axis=0), ada_w, ada_b)
    mod_all = mod_all.reshape(depth, nb_p + nb_s, 6, D)

    dt = x_prompt.dtype
    zeros_p = (jnp.zeros((depth, nb_p, 1, D), dt),
               jnp.zeros((depth, nb_p, RG_CONV_W - 1, D), dt),
               jnp.zeros((depth, nb_p, CC_CONV_W - 1, D), dt))
    state_s = (state_rglru_h.reshape(depth, nb_s, 1, D), state_rglru_conv, state_cconv)

    def trunk(x, mod, states, *, mix_tile, ffn_tile, moe_tile, cc_time_major=False):
        if cc_time_major:
            states = (states[0], states[1], jnp.swapaxes(states[2], 1, 2))
        new_states = None
        for l in range(depth):
            x, new_states = _mixer(l, x, mod[l], states, new_states, w, S=mix_tile[0], T=mix_tile[1],
                                   cc_time_major=cc_time_major)
            fg = final_g if l == depth - 1 else None
            if l % 2 == 0:
                x = _ffn(l, l // 2, x, mod[l], w, fg, S=ffn_tile[0], T=ffn_tile[1])
            else:
                x = _moe(l, l // 2, x, mod[l], w, fg, S=moe_tile[0], T=moe_tile[1])
        h, rc, cc = new_states
        if cc_time_major:
            cc = jnp.swapaxes(cc, 1, 2)
        return x, h[:, :, 0, :], rc, cc

    y_p, h_p, rc_p, cc_p = trunk(x_prompt, mod_all[:, :nb_p], zeros_p,
                                 mix_tile=(1, 256), ffn_tile=(1, 512), moe_tile=(1, MOE_BLOCK))
    y_s, h_s, rc_s, cc_s = trunk(x_sample, mod_all[:, nb_p:], state_s,
                                 mix_tile=(16, 8), ffn_tile=(64, 8), moe_tile=(MOE_BLOCK // 8, 8),
                                 cc_time_major=True)
    return (y_p, y_s, h_p, rc_p, cc_p, h_s, rc_s, cc_s)
```

```python
import functools

import jax
import jax.numpy as jnp
from jax import lax
from jax.experimental import pallas as pl
from jax.experimental.pallas import tpu as pltpu

D_MODEL = 1024
N_RNN_HEADS = 16
RG_CONV_W = 4
RG_C = 8.0
CC_CONV_W = 31
N_EXPERTS = 8
TOP_K = 2
EPS = 1e-6

LANES = 128
SUBLANES = 8
VMEM_LIMIT_BYTES = 56 * 1024 * 1024

GATE_GROUP = 256
RC_HIST = SUBLANES
CC_HIST = 32
CONV_BLOCK = 4

BF16 = jnp.bfloat16
F32 = jnp.float32


def _sigmoid(x):
    return jax.nn.sigmoid(x)


def _silu(x):
    return x * _sigmoid(x)


def _gelu_tanh(x):
    c = 0.7978845608028654
    return 0.5 * x * (1.0 + jnp.tanh(c * (x + 0.044715 * (x * x * x))))


def _softplus(x):
    return jnp.maximum(x, 0.0) + jnp.log1p(jnp.exp(-jnp.abs(x)))


def _rmsnorm(x, g):
    ms = jnp.mean(x * x, axis=-1, keepdims=True)
    return x * lax.rsqrt(ms + EPS) * g


def _dot(a, b):
    return jnp.dot(a, b, preferred_element_type=F32)


def _resident(block_shape, index_map):
    return pl.BlockSpec(block_shape, index_map, pipeline_mode=pl.Buffered(1))


def _ada_kernel(c_ref, w_ref, b_ref, o_ref):
    s = _silu(c_ref[...]).astype(BF16)
    o_ref[...] = _dot(s, w_ref[...].astype(BF16)) + b_ref[...]


def _ada(c_all, ada_w, ada_b):
    depth, d, n = ada_w.shape
    nb = c_all.shape[0]
    tn = 1536
    return pl.pallas_call(
        _ada_kernel,
        grid=(depth, n // tn),
        in_specs=[
            pl.BlockSpec((nb, d), lambda l, j: (0, 0)),
            pl.BlockSpec((None, d, tn), lambda l, j: (l, 0, j)),
            pl.BlockSpec((None, 1, tn), lambda l, j: (l, 0, j)),
        ],
        out_specs=pl.BlockSpec((None, nb, tn), lambda l, j: (l, 0, j)),
        out_shape=jax.ShapeDtypeStruct((depth, nb, n), F32),
        compiler_params=pltpu.CompilerParams(
            dimension_semantics=("arbitrary", "arbitrary"),
            vmem_limit_bytes=VMEM_LIMIT_BYTES),
        name="ada_mod",
    )(c_all, ada_w, ada_b.reshape(depth, 1, n))


LANE_BLOCKS = D_MODEL // LANES


def _rows_to_tile_major(tm_ref, val, lane_block0=0, row_block0=0):
    nb = val.shape[0] // SUBLANES
    for j in range(val.shape[1] // LANES):
        lo = (lane_block0 + j) * SUBLANES
        tm_ref[row_block0:row_block0 + nb, lo:lo + SUBLANES, :] = (
            val[:, j * LANES:(j + 1) * LANES].reshape(nb, SUBLANES, LANES))


def _tile_major_to_rows(tm_ref, row_block0=0, n_blocks=None):
    nb = tm_ref.shape[0] if n_blocks is None else n_blocks
    return jnp.concatenate(
        [tm_ref[row_block0:row_block0 + nb, j * SUBLANES:(j + 1) * SUBLANES, :].reshape(nb * SUBLANES, LANES)
         for j in range(LANE_BLOCKS)], axis=1)


def _load_token(tm_ref, blk, sub):
    return tm_ref[blk, pl.ds(sub, SUBLANES, stride=SUBLANES), :]


def _store_token(tm_ref, blk, sub, val):
    tm_ref[blk, pl.ds(sub, SUBLANES, stride=SUBLANES), :] = val


def _mixer_kernel(x_ref, mod_ref, h0_ref, rc0_ref, cc0_ref,
                  n1g_ref, win_ref, rgw_ref, rgb_ref, gw_ref, ba_ref, bx_ref,
                  lam_ref, wa_ref, ccw_ref, ccb_ref, lng_ref, lnb_ref, wb_ref,
                  bb_ref, wo_ref,
                  xo_ref, ho_ref, rco_ref, cco_ref,
                  hn_s, xbuf, xc_s, a_tm, b_tm, u_tm, y_tm, vt, vtb, hcar, st_tm,
                  *, S, T, n_chunks, cc_time_major, out_layer):
    C = D_MODEL
    stacked_out = (ho_ref, rco_ref, cco_ref)
    if out_layer is not None:
        ho_ref, rco_ref, cco_ref = (r.at[out_layer] for r in stacked_out)
    R = S * T
    HT = CC_HIST + T
    t_idx = pl.program_id(1)

    @pl.when(t_idx == 0)
    def _load_state():
        xbuf[:, RC_HIST - (RG_CONV_W - 1):RC_HIST, :] = rc0_ref[...]
        hist0 = CC_HIST - (CC_CONV_W - 1)
        if cc_time_major:
            for p in range(CC_CONV_W - 1):
                _rows_to_tile_major(st_tm, cc0_ref[p])
                for s in range(S):
                    vt[s * HT + hist0 + p] = _load_token(st_tm, *divmod(s, SUBLANES))
        else:
            pad_rows = jnp.zeros((hist0, C), F32)
            for s in range(S):
                _rows_to_tile_major(st_tm, jnp.concatenate([cc0_ref[s], pad_rows], axis=0))
                for p in range(CC_CONV_W - 1):
                    vt[s * HT + hist0 + p] = _load_token(st_tm, *divmod(p, SUBLANES))
        for s in range(S):
            _rows_to_tile_major(st_tm, jnp.broadcast_to(h0_ref[s], (SUBLANES, C)))
            hcar[s] = _load_token(st_tm, 0, 0)

    x3 = x_ref[...]
    sh1 = mod_ref[:, 0:1, :]
    sc1 = mod_ref[:, 1:2, :]
    g1 = mod_ref[:, 2:3, :]
    hn = _rmsnorm(x3, n1g_ref[...]) * (1.0 + sc1) + sh1
    hn_s[...] = hn.reshape(R, C).astype(BF16)

    def proj(j):
        return _dot(hn_s[...], win_ref[:, j * C:(j + 1) * C])

    xbuf[:, RC_HIST:RC_HIST + T, :] = proj(0).reshape(S, T, C)
    xc3 = rgb_ref[...] + rgw_ref[0:1, :] * xbuf[:, RC_HIST - 3:RC_HIST - 3 + T, :]
    for k in range(1, RG_CONV_W):
        lo = RC_HIST - 3 + k
        xc3 = xc3 + rgw_ref[k:k + 1, :] * xbuf[:, lo:lo + T, :]
    xc_s[...] = xc3.reshape(R, C)

    for g in range(C // GATE_GROUP):
        cs = slice(g * GATE_GROUP, (g + 1) * GATE_GROUP)
        xg = xc_s[:, cs]
        pre = _dot(xg.astype(BF16), gw_ref[g])
        r = _sigmoid(pre[:, :GATE_GROUP] + ba_ref[:, cs])
        i = _sigmoid(pre[:, GATE_GROUP:] + bx_ref[:, cs])
        log_a = (-RG_C * _softplus(-lam_ref[:, cs])) * r
        a = jnp.exp(log_a)
        mult = jnp.sqrt(jnp.tanh(-log_a) * (a * a + 1.0))
        _rows_to_tile_major(a_tm, a, g * (GATE_GROUP // LANES))
        _rows_to_tile_major(b_tm, mult * (i * xg), g * (GATE_GROUP // LANES))

    for s in range(S):
        h = hcar[s]
        for t in range(T):
            blk, sub = divmod(s * T + t, SUBLANES)
            h = _load_token(a_tm, blk, sub) * h + _load_token(b_tm, blk, sub)
            _store_token(b_tm, blk, sub, h)
        hcar[s] = h

    ya_in = (_gelu_tanh(proj(1)) * _tile_major_to_rows(b_tm)).astype(BF16)
    ya = _dot(ya_in, wa_ref[...])

    _rows_to_tile_major(u_tm, proj(2) * _sigmoid(proj(3)))
    for s in range(S):
        for t in range(T):
            blk, sub = divmod(s * T + t, SUBLANES)
            vt[s * HT + CC_HIST + t] = _load_token(u_tm, blk, sub)

    n_streams, Lp = (1, T // 2) if S == 1 else (S // 2, T)
    hi_off = T // 2 if S == 1 else (S // 2) * HT
    HP = CC_HIST + Lp
    first_tap = CC_HIST - (CC_CONV_W - 1)
    for q in range(n_streams):
        for p in range(first_tap, HP):
            lo = q * HT + p
            vtb[q * HP + p] = jnp.concatenate([vt[lo], vt[lo + hi_off]], axis=0).astype(BF16)

    taps = [ccw_ref[k] for k in range(CC_CONV_W)]
    bias = ccb_ref[...]
    blocks_per_stream = Lp // SUBLANES

    def conv_rows(bi, carry):
        q = 0 if n_streams == 1 else bi // blocks_per_stream
        base = bi * SUBLANES + q * CC_HIST + first_tap
        for i0 in range(0, SUBLANES, CONV_BLOCK):
            acc = [None] * CONV_BLOCK
            for j in range(CONV_BLOCK + CC_CONV_W - 1):
                xj = vtb[base + i0 + j].astype(F32)
                for i in range(CONV_BLOCK):
                    k = j - i
                    if 0 <= k < CC_CONV_W:
                        prod = xj * taps[k].astype(F32)
                        acc[i] = prod if acc[i] is None else acc[i] + prod
            for i in range(CONV_BLOCK):
                _store_token(y_tm, bi, i0 + i, acc[i][:SUBLANES] + bias)
                _store_token(y_tm, bi + R // (2 * SUBLANES), i0 + i, acc[i][SUBLANES:] + bias)
        return carry

    lax.fori_loop(0, R // (2 * SUBLANES), conv_rows, 0)

    v = _tile_major_to_rows(y_tm)
    mu = jnp.mean(v, axis=-1, keepdims=True)
    vc = v - mu
    vn = vc * lax.rsqrt(jnp.mean(vc * vc, axis=-1, keepdims=True) + EPS)
    vn = _silu(vn * lng_ref[...] + lnb_ref[...])
    yb = _dot(vn.astype(BF16), wb_ref[...]) + bb_ref[...]

    m = _sigmoid(proj(4)) * ya + _sigmoid(proj(5)) * yb
    o = _dot(m.astype(BF16), wo_ref[...])
    xo_ref[...] = x3 + (1.0 + g1) * o.reshape(S, T, C)

    @pl.when(t_idx == n_chunks - 1)
    def _write_state():
        if out_layer is not None:
            for r in stacked_out:
                for li in range(r.shape[0]):
                    if li != out_layer:
                        r[li] = jnp.zeros(r.shape[1:], F32)
        rco_ref[...] = xbuf[:, RC_HIST + T - (RG_CONV_W - 1):RC_HIST + T, :]
        for s in range(S):
            _store_token(st_tm, 0, 0, hcar[s])
            ho_ref[s] = _tile_major_to_rows(st_tm, 0, 1)[0:1, :]
        if cc_time_major:
            for p in range(CC_CONV_W - 1):
                for s in range(S):
                    _store_token(st_tm, *divmod(s, SUBLANES), vt[s * HT + HT - (CC_CONV_W - 1) + p])
                cco_ref[p] = _tile_major_to_rows(st_tm, 0, S // SUBLANES)
        else:
            for s in range(S):
                for p in range(CC_CONV_W - 1):
                    _store_token(st_tm, *divmod(p, SUBLANES), vt[s * HT + HT - (CC_CONV_W - 1) + p])
                cco_ref[s] = _tile_major_to_rows(st_tm)[0:CC_CONV_W - 1, :]

    if n_chunks > 1:
        xbuf[:, 0:RC_HIST, :] = xbuf[:, T:T + RC_HIST, :]
        for s in range(S):
            vt[s * HT:s * HT + CC_HIST] = vt[s * HT + T:s * HT + T + CC_HIST]


N_MIXER_INPUTS = 21


def _mixer_kernel_with_aliases(*refs, n_alias, **static):
    _mixer_kernel(*refs[:N_MIXER_INPUTS], *refs[N_MIXER_INPUTS + n_alias:], **static)


def _mixer(l, x, mod, states, prev_states, w, *, S, T, cc_time_major=False):
    h0, rc0, cc0 = states
    assert cc0.shape[1:3] == ((CC_CONV_W - 1, x.shape[0]) if cc_time_major else (x.shape[0], CC_CONV_W - 1))
    B, Tt, D = x.shape
    C = D
    n_chunks = Tt // T
    assert B % S == 0 and Tt % T == 0 and T % SUBLANES == 0
    assert S == 1 or n_chunks == 1
    assert n_chunks == 1 or T >= CC_HIST
    R = S * T

    def seq(*tail):
        return pl.BlockSpec((S,) + tail, lambda b, t: (b,) + (0,) * len(tail))

    def state(*tail):
        return pl.BlockSpec((None, S) + tail, lambda b, t: (l, b) + (0,) * len(tail))

    n_layers = None if prev_states is not None else h0.shape[0]

    def out_state(*tail):
        return pl.BlockSpec((n_layers, S) + tail, lambda b, t: (0 if n_layers else l, b) + (0,) * len(tail))

    if cc_time_major:
        assert S % SUBLANES == 0 and S // SUBLANES <= CC_HIST // SUBLANES
        cc_spec = pl.BlockSpec((None, CC_CONV_W - 1, S, C), lambda b, t: (l, 0, b, 0))
        cc_out_spec = pl.BlockSpec((n_layers, CC_CONV_W - 1, S, C), lambda b, t: (0 if n_layers else l, 0, b, 0))
    else:
        cc_spec = state(CC_CONV_W - 1, C)
        cc_out_spec = out_state(CC_CONV_W - 1, C)

    def layer(*tail):
        return _resident((None,) + tail, lambda b, t: (l,) + (0,) * len(tail))

    in_specs = [
        pl.BlockSpec((S, T, D), lambda b, t: (b, t, 0)),
        seq(6, D),
        state(1, C), state(RG_CONV_W - 1, C), cc_spec,
        layer(1, D),
        layer(D, 6 * C),
        layer(RG_CONV_W, C), layer(1, C),
        layer(C // GATE_GROUP, GATE_GROUP, 2 * GATE_GROUP),
        layer(1, C), layer(1, C), layer(1, C),
        layer(C, D),
        layer(CC_CONV_W, 2 * SUBLANES, LANES), layer(SUBLANES, LANES),
        layer(1, C), layer(1, C),
        layer(C, D), layer(1, D),
        layer(D, D),
    ]
    out_specs = [
        pl.BlockSpec((S, T, D), lambda b, t: (b, t, 0)),
        out_state(1, C), out_state(RG_CONV_W - 1, C), cc_out_spec,
    ]
    out_shape = [jax.ShapeDtypeStruct((B, Tt, D), F32)] + [
        jax.ShapeDtypeStruct(a.shape, F32) for a in states]
    aliases = {}
    alias_args = []
    if prev_states is not None:
        alias_args = list(prev_states)
        in_specs += [pl.BlockSpec(memory_space=pl.ANY)] * len(alias_args)
        aliases = {N_MIXER_INPUTS + k: 1 + k for k in range(len(alias_args))}
    tile_major = pltpu.VMEM((R // SUBLANES, LANE_BLOCKS * SUBLANES, LANES), F32)
    scratch = [
        pltpu.VMEM((R, D), BF16),
        pltpu.VMEM((S, RC_HIST + T, C), F32),
        pltpu.VMEM((R, C), F32),
        tile_major, tile_major, tile_major, tile_major,
        pltpu.VMEM((S * (CC_HIST + T), SUBLANES, LANES), F32),
        pltpu.VMEM((R // 2 + CC_HIST * (1 if S == 1 else S // 2), 2 * SUBLANES, LANES), BF16),
        pltpu.VMEM((S, SUBLANES, LANES), F32),
        pltpu.VMEM((CC_HIST // SUBLANES, LANE_BLOCKS * SUBLANES, LANES), F32),
    ]
    args = [x, mod, h0, rc0, cc0,
            w["norm1_g"], w["w_in"], w["rg_conv_w"], w["rg_conv_b"], w["gate_w"],
            w["rg_ba"], w["rg_bx"], w["rg_lambda"], w["w_branch_a"],
            w["cc_dw_w"], w["cc_dw_b"], w["cc_ln_g"], w["cc_ln_b"], w["w_branch_b"],
            w["b_branch_b"], w["w_out"]]
    assert len(args) == N_MIXER_INPUTS
    outs = pl.pallas_call(
        functools.partial(_mixer_kernel_with_aliases, n_alias=len(alias_args), S=S, T=T, n_chunks=n_chunks,
                          cc_time_major=cc_time_major, out_layer=l if prev_states is None else None),
        grid=(B // S, n_chunks),
        in_specs=in_specs,
        out_specs=out_specs,
        out_shape=out_shape,
        scratch_shapes=scratch,
        input_output_aliases=aliases,
        compiler_params=pltpu.CompilerParams(
            dimension_semantics=("arbitrary", "arbitrary"),
            vmem_limit_bytes=VMEM_LIMIT_BYTES),
        name=f"mixer_S{S}",
    )(*args, *alias_args)
    return outs[0], tuple(outs[1:])


def _finish(x3, g2, f, fg_ref, o_ref):
    y = x3 + (1.0 + g2) * f.reshape(x3.shape)
    if fg_ref is not None:
        y = _rmsnorm(y, fg_ref[...])
    o_ref[...] = y


def _ffn_kernel(x_ref, mod_ref, n2g_ref, wg_ref, wu_ref, wd_ref, *rest, n_split, final):
    fg_ref = rest[0] if final else None
    o_ref = rest[-1]
    x3 = x_ref[...]
    S, T, D = x3.shape
    sh2 = mod_ref[:, 3:4, :]
    sc2 = mod_ref[:, 4:5, :]
    g2 = mod_ref[:, 5:6, :]
    hn = (_rmsnorm(x3, n2g_ref[...]) * (1.0 + sc2) + sh2).reshape(S * T, D).astype(BF16)
    ff = wg_ref.shape[1] // n_split
    f = None
    for j in range(n_split):
        cs = slice(j * ff, (j + 1) * ff)
        h = (_silu(_dot(hn, wg_ref[:, cs])) * _dot(hn, wu_ref[:, cs])).astype(BF16)
        part = _dot(h, wd_ref[cs, :])
        f = part if f is None else f + part
    _finish(x3, g2, f, fg_ref, o_ref)


def _ffn(l, j, x, mod, w, final_g, *, S, T):
    B, Tt, D = x.shape
    ff = w["ffn_w_gate"].shape[-1]
    final = final_g is not None
    in_specs = [
        pl.BlockSpec((S, T, D), lambda b, t: (b, t, 0)),
        pl.BlockSpec((S, 6, D), lambda b, t: (b, 0, 0)),
        _resident((None, 1, D), lambda b, t: (l, 0, 0)),
        _resident((None, D, ff), lambda b, t: (j, 0, 0)),
        _resident((None, D, ff), lambda b, t: (j, 0, 0)),
        _resident((None, ff, D), lambda b, t: (j, 0, 0)),
    ]
    args = [x, mod, w["norm2_g"], w["ffn_w_gate"], w["ffn_w_up"], w["ffn_w_down"]]
    if final:
        in_specs.append(_resident((1, D), lambda b, t: (0, 0)))
        args.append(final_g)
    return pl.pallas_call(
        functools.partial(_ffn_kernel, n_split=2, final=final),
        grid=(B // S, Tt // T),
        in_specs=in_specs,
        out_specs=pl.BlockSpec((S, T, D), lambda b, t: (b, t, 0)),
        out_shape=jax.ShapeDtypeStruct((B, Tt, D), F32),
        compiler_params=pltpu.CompilerParams(
            dimension_semantics=("arbitrary", "arbitrary"),
            vmem_limit_bytes=VMEM_LIMIT_BYTES),
        name=f"ffn_S{S}",
    )(*args)


MOE_BLOCK = 1024
MOE_ROWS = 128
MOE_SLOTS = TOP_K * MOE_BLOCK + N_EXPERTS * MOE_ROWS


MOE_FINISH_ROWS = 256


def _adaln2(x3, mod_ref, n2g_ref, rows):
    return _rmsnorm(x3, n2g_ref[...]) * (1.0 + mod_ref[rows, 4:5, :]) + mod_ref[rows, 3:4, :]


def _route_kernel(x_ref, mod_ref, n2g_ref, wr_ref, br_ref, tri_ref, meta_ref, info_ref):
    x3 = x_ref[...]
    S, T, D = x3.shape
    R = S * T
    hn = _adaln2(x3, mod_ref, n2g_ref, slice(None)).reshape(R, D)
    hn_hi = hn.astype(BF16)
    hn_lo = (hn - hn_hi.astype(F32)).astype(BF16)
    both = _dot(hn_hi, wr_ref[...])
    logits = both[:, :LANES] + both[:, LANES:] + _dot(hn_lo, wr_ref[:, :LANES]) + br_ref[...]
    z = logits - jnp.max(logits, axis=-1, keepdims=True)
    p = jnp.exp(z)
    p = p / jnp.sum(p, axis=-1, keepdims=True)
    lane_i = lax.broadcasted_iota(jnp.int32, p.shape, 1)
    lane = lane_i.astype(F32)
    far = jnp.float32(p.shape[1])
    m1 = jnp.max(p, axis=-1, keepdims=True)
    i1 = jnp.min(jnp.where(p == m1, lane, far), axis=-1, keepdims=True)
    pick1 = lane == i1
    p2 = jnp.where(pick1, -1.0, p)
    m2 = jnp.max(p2, axis=-1, keepdims=True)
    i2 = jnp.min(jnp.where(p2 == m2, lane, far), axis=-1, keepdims=True)
    pick2 = lane == i2
    onehot = jnp.where(pick1, 1.0, 0.0) + jnp.where(pick2, 1.0, 0.0)
    rank = _dot(tri_ref[...], onehot.astype(BF16))
    count = rank[R - 1:R, :] + onehot[R - 1:R, :]
    count8 = jnp.broadcast_to(count, (SUBLANES, LANES))
    padded = jnp.floor((count8 + (MOE_ROWS - 1)) * (1.0 / MOE_ROWS)) * MOE_ROWS
    incl = padded
    lane8 = lax.broadcasted_iota(jnp.int32, (SUBLANES, LANES), 1)
    for k in (1, 2, 4):
        incl = incl + jnp.where(lane8 >= k, pltpu.roll(incl, k, 1), 0.0)
    start8 = incl - padded
    slot = start8[0:1, :] + rank
    slot1 = jnp.sum(jnp.where(pick1, slot, 0.0), axis=-1, keepdims=True)
    slot2 = jnp.sum(jnp.where(pick2, slot, 0.0), axis=-1, keepdims=True)
    den = m1 + m2
    meta_ref[...] = jnp.where(lane_i == 0, slot1, jnp.where(lane_i == 1, slot2,
                              jnp.where(lane_i == 2, m1 / den, jnp.where(lane_i == 3, m2 / den, 0.0))))
    sub8 = lax.broadcasted_iota(jnp.int32, (SUBLANES, LANES), 0)
    info_ref[...] = jnp.where(sub8 == 0, start8, jnp.where(sub8 == 1, count8, 0.0))


def _route(l, j, x, mod, w, *, S, T):
    B, Tt, D = x.shape
    R = S * T
    assert R == MOE_BLOCK
    nt = Tt // T
    nblk = (B // S) * nt
    return pl.pallas_call(
        _route_kernel,
        grid=(B // S, nt),
        in_specs=[
            pl.BlockSpec((S, T, D), lambda b, t: (b, t, 0)),
            pl.BlockSpec((S, 6, D), lambda b, t: (b, 0, 0)),
            _resident((None, 1, D), lambda b, t: (l, 0, 0)),
            _resident((None, D, 2 * LANES), lambda b, t: (j, 0, 0)),
            _resident((None, 1, LANES), lambda b, t: (j, 0, 0)),
            _resident((R, R), lambda b, t: (0, 0)),
        ],
        out_specs=[
            pl.BlockSpec((None, R, LANES), lambda b, t: (b * nt + t, 0, 0)),
            pl.BlockSpec((None, SUBLANES, LANES), lambda b, t: (b * nt + t, 0, 0)),
        ],
        out_shape=[
            jax.ShapeDtypeStruct((nblk, R, LANES), F32),
            jax.ShapeDtypeStruct((nblk, SUBLANES, LANES), F32),
        ],
        compiler_params=pltpu.CompilerParams(
            dimension_semantics=("arbitrary", "arbitrary"),
            vmem_limit_bytes=VMEM_LIMIT_BYTES),
        name=f"route_S{S}",
    )(x, mod, w["norm2_g"], w["router_w"], w["router_b"], jnp.tril(jnp.ones((R, R), BF16), -1))


def _row_chunks(S, T, rows):
    if S == 1:
        return [(slice(0, 1), slice(r0, r0 + rows), r0) for r0 in range(0, T, rows)]
    per = rows // T
    return [(slice(s0, s0 + per), slice(None), s0 * T) for s0 in range(0, S, per)]


def _moe_kernel(start_ref, count_ref, x_ref, mod_ref, n2g_ref, slot_ref, gate_ref, wg_ref, wu_ref, wd_ref,
                *rest, final):
    fg_ref = rest[0] if final else None
    o_ref, hn_tm, xs_tm, y_tm, ys, inv = rest[-6:]
    b = pl.program_id(0)
    e = pl.program_id(1)
    S, T, D = x_ref.shape
    TB = S * T
    pieces = _row_chunks(S, T, MOE_FINISH_ROWS)

    @pl.when(e == 0)
    def _prepare():
        for ss, ts, r0 in pieces:
            hn = _adaln2(x_ref[ss, ts, :], mod_ref, n2g_ref, ss)
            _rows_to_tile_major(hn_tm, hn.reshape(MOE_FINISH_ROWS, D), 0, r0 // SUBLANES)

        def clear(i, c):
            inv[i] = 0
            return c
        for ex in range(N_EXPERTS):
            cnt = count_ref[b * N_EXPERTS + ex]
            lo = start_ref[b * N_EXPERTS + ex] + cnt
            lax.fori_loop(lo, lo + lax.rem(MOE_ROWS - lax.rem(cnt, MOE_ROWS), MOE_ROWS), clear, 0)

        def fill(i, c):
            inv[slot_ref[0, TOP_K * i]] = i
            inv[slot_ref[0, TOP_K * i + 1]] = i
            return c
        lax.fori_loop(0, TB, fill, 0, unroll=SUBLANES)

    first = start_ref[b * N_EXPERTS + e]
    n_rows = count_ref[b * N_EXPERTS + e]
    n_chunks = lax.div(n_rows + (MOE_ROWS - 1), MOE_ROWS)

    last = N_EXPERTS - 1
    last_base = (start_ref[b * N_EXPERTS + last]
                 + lax.div(count_ref[b * N_EXPERTS + last] + (MOE_ROWS - 1), MOE_ROWS) * MOE_ROWS - MOE_ROWS)

    def gather(base):
        dst = xs_tm.at[lax.bitwise_and(lax.div(base, MOE_ROWS), 1)]
        for r in range(MOE_ROWS):
            t = inv[base + r]
            _store_token(dst, r // SUBLANES, r % SUBLANES,
                         _load_token(hn_tm, lax.shift_right_logical(t, 3), lax.bitwise_and(t, 7)))

    @pl.when(e == 0)
    def _first_gather():
        gather(0)

    def chunk(c, carry):
        base = first + c * MOE_ROWS
        lhs = _tile_major_to_rows(xs_tm.at[lax.bitwise_and(lax.div(base, MOE_ROWS), 1)]).astype(BF16)
        gather(jnp.minimum(base + MOE_ROWS, last_base))
        h = (_silu(_dot(lhs, wg_ref[...])) * _dot(lhs, wu_ref[...])).astype(BF16)
        _rows_to_tile_major(y_tm, _dot(h, wd_ref[...]))
        for r in range(MOE_ROWS):
            ys[base + r] = _load_token(y_tm, r // SUBLANES, r % SUBLANES)
        return carry

    lax.fori_loop(0, n_chunks, chunk, 0)

    @pl.when(e == N_EXPERTS - 1)
    def _combine():
        def comb(i8, c):
            for i in range(SUBLANES):
                t = i8 * SUBLANES + i
                f = (gate_ref[0, TOP_K * t] * ys[slot_ref[0, TOP_K * t]]
                     + gate_ref[0, TOP_K * t + 1] * ys[slot_ref[0, TOP_K * t + 1]])
                _store_token(hn_tm, i8, i, f)
            return c
        lax.fori_loop(0, TB // SUBLANES, comb, 0)
        for ss, ts, r0 in pieces:
            x3 = x_ref[ss, ts, :]
            f = _tile_major_to_rows(hn_tm, r0 // SUBLANES, MOE_FINISH_ROWS // SUBLANES)
            y = x3 + (1.0 + mod_ref[ss, 5:6, :]) * f.reshape(x3.shape)
            if final:
                y = _rmsnorm(y, fg_ref[...])
            o_ref[ss, ts, :] = y


def _moe(l, j, x, mod, w, final_g, *, S, T):
    B, Tt, D = x.shape
    ff = w["moe_w_gate"].shape[-1]
    final = final_g is not None
    TB = S * T
    nt = Tt // T
    nblk = (B // S) * nt
    meta, info = _route(l, j, x, mod, w, S=S, T=T)

    slots = meta[:, :, 0:TOP_K].astype(jnp.int32).reshape(nblk, 1, TOP_K * TB)
    gates = meta[:, :, TOP_K:2 * TOP_K].reshape(nblk, 1, TOP_K * TB)
    starts = info[:, 0, :N_EXPERTS].astype(jnp.int32).reshape(nblk * N_EXPERTS)
    counts = info[:, 1, :N_EXPERTS].astype(jnp.int32).reshape(nblk * N_EXPERTS)

    smem_row = lambda: pl.BlockSpec((None, 1, TOP_K * TB), lambda b, e, st, ct: (b, 0, 0), memory_space=pltpu.SMEM)
    in_specs = [
        pl.BlockSpec((S, T, D), lambda b, e, st, ct: (b // nt, b % nt, 0)),
        pl.BlockSpec((S, 6, D), lambda b, e, st, ct: (b // nt, 0, 0)),
        pl.BlockSpec((None, 1, D), lambda b, e, st, ct: (l, 0, 0)),
        smem_row(), smem_row(),
        pl.BlockSpec((None, None, D, ff), lambda b, e, st, ct: (j, e, 0, 0)),
        pl.BlockSpec((None, None, D, ff), lambda b, e, st, ct: (j, e, 0, 0)),
        pl.BlockSpec((None, None, ff, D), lambda b, e, st, ct: (j, e, 0, 0)),
    ]
    args = [x, mod, w["norm2_g"], slots, gates, w["moe_w_gate"], w["moe_w_up"], w["moe_w_down"]]
    if final:
        in_specs.append(pl.BlockSpec((1, D), lambda b, e, st, ct: (0, 0)))
        args.append(final_g)
    tile_major = lambda rows: pltpu.VMEM((rows // SUBLANES, LANE_BLOCKS * SUBLANES, LANES), F32)
    return pl.pallas_call(
        functools.partial(_moe_kernel, final=final),
        grid_spec=pltpu.PrefetchScalarGridSpec(
            num_scalar_prefetch=2,
            grid=(nblk, N_EXPERTS),
            in_specs=in_specs,
            out_specs=pl.BlockSpec((S, T, D), lambda b, e, st, ct: (b // nt, b % nt, 0)),
            scratch_shapes=[
                tile_major(TB),
                pltpu.VMEM((2, MOE_ROWS // SUBLANES, LANE_BLOCKS * SUBLANES, LANES), F32),
                tile_major(MOE_ROWS),
                pltpu.VMEM((MOE_SLOTS, SUBLANES, LANES), F32),
                pltpu.SMEM((MOE_SLOTS,), jnp.int32),
            ]),
        out_shape=jax.ShapeDtypeStruct((B, Tt, D), F32),
        compiler_params=pltpu.CompilerParams(
            dimension_semantics=("arbitrary", "arbitrary"),
            vmem_limit_bytes=VMEM_LIMIT_BYTES),
        name=f"moe_S{S}",
    )(starts, counts, *args)


def _block_diag_gate_weights(wa, wx):
    def bd(w):
        L, H, bs, _ = w.shape
        hg = GATE_GROUP // bs
        w5 = w.reshape(L, H // hg, hg, bs, bs)
        eye = jnp.eye(hg, dtype=w.dtype)
        m = jnp.einsum("lqhij,hk->lqhikj", w5, eye)
        return m.reshape(L, H // hg, GATE_GROUP, GATE_GROUP)
    return jnp.concatenate([bd(wa), bd(wx)], axis=-1)


def kernel(x_prompt, x_sample, c_prompt, c_sample, state_rglru_h, state_rglru_conv, state_cconv,
           ada_w, ada_b, norm1_g, norm2_g, w_in, rg_conv_w, rg_conv_b, rg_wa, rg_ba, rg_wx, rg_bx,
           rg_lambda, w_branch_a, cc_dw_w, cc_dw_b, cc_ln_g, cc_ln_b, w_branch_b, b_branch_b, w_out,
           ffn_w_gate, ffn_w_up, ffn_w_down, moe_router_w, moe_router_b, moe_w_gate, moe_w_up,
           moe_w_down, final_norm_g):
    depth = ada_w.shape[0]
    D = D_MODEL
    row = lambda a: a.reshape(a.shape[0], 1, a.shape[-1])
    n_moe = moe_router_w.shape[0]
    router_w = jnp.zeros((n_moe, D, LANES), F32).at[:, :, :N_EXPERTS].set(moe_router_w)
    router_hi = router_w.astype(BF16)
    router_lo = (router_w - router_hi.astype(F32)).astype(BF16)
    router_w = jnp.concatenate([router_hi, router_lo], axis=-1)
    router_b = jnp.full((n_moe, 1, LANES), -1e30, F32).at[:, 0, :N_EXPERTS].set(moe_router_b)
    w = {
        "norm1_g": row(norm1_g), "norm2_g": row(norm2_g),
        "w_in": w_in.astype(BF16),
        "rg_conv_w": rg_conv_w, "rg_conv_b": row(rg_conv_b),
        "gate_w": _block_diag_gate_weights(rg_wa, rg_wx).astype(BF16),
        "rg_ba": row(rg_ba), "rg_bx": row(rg_bx), "rg_lambda": row(rg_lambda),
        "w_branch_a": w_branch_a.astype(BF16),
        "cc_dw_w": jnp.tile(cc_dw_w.reshape(depth, CC_CONV_W, SUBLANES, LANES), (1, 1, 2, 1)).astype(BF16),
        "cc_dw_b": cc_dw_b.reshape(depth, SUBLANES, LANES),
        "cc_ln_g": row(cc_ln_g), "cc_ln_b": row(cc_ln_b),
        "w_branch_b": w_branch_b.astype(BF16), "b_branch_b": row(b_branch_b),
        "w_out": w_out.astype(BF16),
        "ffn_w_gate": ffn_w_gate.astype(BF16), "ffn_w_up": ffn_w_up.astype(BF16),
        "ffn_w_down": ffn_w_down.astype(BF16),
        "router_w": router_w, "router_b": router_b,
        "moe_w_gate": moe_w_gate.astype(BF16), "moe_w_up": moe_w_up.astype(BF16),
        "moe_w_down": moe_w_down.astype(BF16),
    }
    final_g = final_norm_g.reshape(1, D)

    nb_p = x_prompt.shape[0]
    nb_s = x_sample.shape[0]
    mod_all = _ada(jnp.concatenate([c_prompt, c_sample], axis=0), ada_w, ada_b)
    mod_all = mod_all.reshape(depth, nb_p + nb_s, 6, D)

    dt = x_prompt.dtype
    zeros_p = (jnp.zeros((depth, nb_p, 1, D), dt),
               jnp.zeros((depth, nb_p, RG_CONV_W - 1, D), dt),
               jnp.zeros((depth, nb_p, CC_CONV_W - 1, D), dt))
    state_s = (state_rglru_h.reshape(depth, nb_s, 1, D), state_rglru_conv, state_cconv)

    def trunk(x, mod, states, *, mix_tile, ffn_tile, moe_tile, cc_time_major=False):
        if cc_time_major:
            states = (states[0], states[1], jnp.swapaxes(states[2], 1, 2))
        new_states = None
        for l in range(depth):
            x, new_states = _mixer(l, x, mod[l], states, new_states, w, S=mix_tile[0], T=mix_tile[1],
                                   cc_time_major=cc_time_major)
            fg = final_g if l == depth - 1 else None
            if l % 2 == 0:
                x = _ffn(l, l // 2, x, mod[l], w, fg, S=ffn_tile[0], T=ffn_tile[1])
            else:
                x = _moe(l, l // 2, x, mod[l], w, fg, S=moe_tile[0], T=moe_tile[1])
        h, rc, cc = new_states
        if cc_time_major:
            cc = jnp.swapaxes(cc, 1, 2)
        return x, h[:, :, 0, :], rc, cc

    y_p, h_p, rc_p, cc_p = trunk(x_prompt, mod_all[:, :nb_p], zeros_p,
                                 mix_tile=(1, 512), ffn_tile=(1, 1024), moe_tile=(1, MOE_BLOCK))
    y_s, h_s, rc_s, cc_s = trunk(x_sample, mod_all[:, nb_p:], state_s,
                                 mix_tile=(16, 8), ffn_tile=(64, 8), moe_tile=(MOE_BLOCK // 8, 8),
                                 cc_time_major=True)
    return (y_p, y_s, h_p, rc_p, cc_p, h_s, rc_s, cc_s)
```
